```python
import math
import jax, jax.numpy as jnp
from jax import lax
import numpy as np

D_MODEL = 1024
BATCH = 4
SEQ = 4096
DEPTH = 4
DEC_BATCH = 2
DEC_SEQ = 8192
PAST_LEN = 128

GRID_W = 64
Q_BLOCK = 128
EPS = 1e-6
ROPE_THETA = 10000.0
MLA_HEADS = 8
MLA_Q_RANK = 256
MLA_KV_RANK = 128
MLA_NOPE_DIM = 64
MLA_ROPE_DIM = 32
MLA_V_DIM = 64
MLA_QK_DIM = MLA_NOPE_DIM + MLA_ROPE_DIM
MLA_WIDTH = MLA_HEADS * MLA_V_DIM
MLA_SCALE = 1.0 / math.sqrt(MLA_QK_DIM)
GQA_HEADS = 8
GQA_KV_HEADS = 2
GQA_HEAD_DIM = 64
GQA_GROUP = GQA_HEADS // GQA_KV_HEADS
GQA_WIDTH = GQA_HEADS * GQA_HEAD_DIM
GQA_KV_WIDTH = GQA_KV_HEADS * GQA_HEAD_DIM
GQA_SCALE = 1.0 / math.sqrt(GQA_HEAD_DIM)
D_MIX = MLA_WIDTH + GQA_WIDTH
IN_SPLIT_SIZES = (MLA_Q_RANK, MLA_KV_RANK, MLA_ROPE_DIM, MLA_WIDTH,
                  GQA_WIDTH, GQA_KV_WIDTH, GQA_KV_WIDTH, GQA_WIDTH)
D_IN_PROJ = sum(IN_SPLIT_SIZES)

kernel_name = "hybrid_mla_axial_gqa_encoder"


def rms_norm(x, g):
    xf = x.astype(jnp.float32)
    y = xf * lax.rsqrt(jnp.mean(xf * xf, axis=-1, keepdims=True) + EPS)
    return (y * g.astype(jnp.float32)).astype(x.dtype)


def rotate(x, pos):
    d = x.shape[-1]
    half = d // 2
    inv = ROPE_THETA ** (-jnp.arange(half, dtype=jnp.float32) / half)
    ang = pos.astype(jnp.float32)[:, None] * inv[None, :]
    cos, sin = jnp.cos(ang), jnp.sin(ang)
    xf = x.astype(jnp.float32)
    x1, x2 = xf[..., :half], xf[..., half:]
    return jnp.concatenate([x1 * cos - x2 * sin, x1 * sin + x2 * cos], axis=-1).astype(x.dtype)


def axial_rotate(x, row, col):
    h = x.shape[-1] // 2
    return jnp.concatenate([rotate(x[..., :h], row), rotate(x[..., h:], col)], axis=-1)


def block_attention(q, k, v, scale):
    B, Hk, G, S, dk = q.shape
    dv = v.shape[-1]
    nb = S // Q_BLOCK
    qb = q.reshape(B, Hk, G, nb, Q_BLOCK, dk).transpose(3, 0, 1, 2, 4, 5)

    def one(qblk):
        s = jnp.einsum('bkgqd,bksd->bkgqs', qblk, k, preferred_element_type=jnp.float32) * scale
        p = jax.nn.softmax(s, axis=-1).astype(v.dtype)
        return jnp.einsum('bkgqs,bksd->bkgqd', p, v)

    o = lax.map(one, qb)
    return o.transpose(1, 2, 3, 0, 4, 5).reshape(B, Hk * G, S, dv)


def heads_to_tokens(o):
    B, H, S, d = o.shape
    return o.transpose(0, 2, 1, 3).reshape(B, S, H * d)


def mixer_layer(x, c, t, row, col, w_ada, b_ada, g_norm, w_in, g_qa, w_uq, g_kva, w_ukv, g_qn, g_kn, w_out):
    B, S, _ = x.shape
    mod = jax.nn.silu(c) @ w_ada + b_ada
    shift, scale, gate = jnp.split(mod, 3, axis=-1)
    h = rms_norm(x, g_norm) * (1 + scale[:, None, :]) + shift[:, None, :]

    proj = h @ w_in
    parts = []
    off = 0
    for sz in IN_SPLIT_SIZES:
        parts.append(proj[..., off:off + sz])
        off += sz
    q_lat, kv_lat, k_rope, z_a, q_b, k_b, v_b, z_b = parts

    qa = (rms_norm(q_lat, g_qa) @ w_uq).reshape(B, S, MLA_HEADS, MLA_QK_DIM).transpose(0, 2, 1, 3)
    qa = jnp.concatenate([qa[..., :MLA_NOPE_DIM], rotate(qa[..., MLA_NOPE_DIM:], t)], axis=-1)
    kv = (rms_norm(kv_lat, g_kva) @ w_ukv).reshape(B, S, MLA_HEADS, MLA_NOPE_DIM + MLA_V_DIM).transpose(0, 2, 1, 3)
    k_nope, va = kv[..., :MLA_NOPE_DIM], kv[..., MLA_NOPE_DIM:]
    k_pe = rotate(k_rope, t)[:, None, :, :]
    ka = jnp.concatenate([k_nope, jnp.broadcast_to(k_pe, (B, MLA_HEADS, S, MLA_ROPE_DIM))], axis=-1)
    oa = block_attention(qa[:, :, None], ka, va, MLA_SCALE)
    ya = heads_to_tokens(oa) * jax.nn.silu(z_a)

    qb = q_b.reshape(B, S, GQA_HEADS, GQA_HEAD_DIM).transpose(0, 2, 1, 3)
    kb = k_b.reshape(B, S, GQA_KV_HEADS, GQA_HEAD_DIM).transpose(0, 2, 1, 3)
    vb = v_b.reshape(B, S, GQA_KV_HEADS, GQA_HEAD_DIM).transpose(0, 2, 1, 3)
    qb = axial_rotate(rms_norm(qb, g_qn), row, col)
    kb = axial_rotate(rms_norm(kb, g_kn), row, col)
    qb = qb.reshape(B, GQA_KV_HEADS, GQA_GROUP, S, GQA_HEAD_DIM)
    ob = block_attention(qb, kb, vb, GQA_SCALE)
    yb = heads_to_tokens(ob) * jax.nn.silu(z_b)

    out = jnp.concatenate([ya, yb], axis=-1) @ w_out
    return x + gate[:, None, :] * out


def trunk(x, c, w_ada, b_ada, g_norm, w_in, g_qa, w_uq, g_kva, w_ukv, g_qn, g_kn, w_out, g_final):
    S = x.shape[1]
    rows = S // GRID_W
    t = jnp.arange(S, dtype=jnp.int32)
    row = jnp.repeat(jnp.arange(rows, dtype=jnp.int32), GRID_W)
    col = jnp.tile(jnp.arange(GRID_W, dtype=jnp.int32), rows)
    for l in range(DEPTH):
        x = mixer_layer(x, c, t, row, col, w_ada[l], b_ada[l], g_norm[l], w_in[l], g_qa[l], w_uq[l],
                        g_kva[l], w_ukv[l], g_qn[l], g_kn[l], w_out[l])
    return rms_norm(x, g_final)


def setup_inputs(seed: int = 0) -> dict:
    key = jax.random.key(seed)
    ks = jax.random.split(key, 20)
    f32 = jnp.float32

    def w(k, shape, fan_in, mult=1.0):
        return jax.random.normal(k, shape, f32) * (mult * fan_in ** -0.5)

    def gain(k, shape):
        return 1.0 + 0.02 * jax.random.normal(k, shape, f32)

    return {
        "x_prompt": jax.random.normal(ks[0], (BATCH, SEQ, D_MODEL), f32),
        "x_sample": jax.random.normal(ks[1], (DEC_BATCH, DEC_SEQ, D_MODEL), f32),
        "c_prompt": jax.random.normal(ks[2], (BATCH, D_MODEL), f32),
        "c_sample": jax.random.normal(ks[3], (DEC_BATCH, D_MODEL), f32),
        "w_ada": w(ks[4], (DEPTH, D_MODEL, 3 * D_MODEL), D_MODEL, 0.5),
        "b_ada": 0.02 * jax.random.normal(ks[5], (DEPTH, 3 * D_MODEL), f32),
        "g_norm": gain(ks[6], (DEPTH, D_MODEL)),
        "w_in": w(ks[7], (DEPTH, D_MODEL, D_IN_PROJ), D_MODEL),
        "g_qa": gain(ks[8], (DEPTH, MLA_Q_RANK)),
        "w_uq": w(ks[9], (DEPTH, MLA_Q_RANK, MLA_HEADS * MLA_QK_DIM), MLA_Q_RANK),
        "g_kva": gain(ks[10], (DEPTH, MLA_KV_RANK)),
        "w_ukv": w(ks[11], (DEPTH, MLA_KV_RANK, MLA_HEADS * (MLA_NOPE_DIM + MLA_V_DIM)), MLA_KV_RANK),
        "g_qn": gain(ks[12], (DEPTH, GQA_HEAD_DIM)),
        "g_kn": gain(ks[13], (DEPTH, GQA_HEAD_DIM)),
        "w_out": w(ks[14], (DEPTH, D_MIX, D_MODEL), D_MIX),
        "g_final": gain(ks[15], (D_MODEL,)),
    }


def reference(x_prompt, x_sample, c_prompt, c_sample, w_ada, b_ada, g_norm, w_in, g_qa, w_uq,
              g_kva, w_ukv, g_qn, g_kn, w_out, g_final):
    y_prompt = trunk(x_prompt, c_prompt, w_ada, b_ada, g_norm, w_in, g_qa, w_uq, g_kva, w_ukv,
                     g_qn, g_kn, w_out, g_final)
    y_sample = trunk(x_sample, c_sample, w_ada, b_ada, g_norm, w_in, g_qa, w_uq, g_kva, w_ukv,
                     g_qn, g_kn, w_out, g_final)
    return (y_prompt, y_sample)
```

```python
import functools
import math

import numpy as np
import jax
import jax.numpy as jnp
from jax import lax
from jax.experimental import pallas as pl
from jax.experimental.pallas import tpu as pltpu

D_MODEL = 1024
DEPTH = 4
GRID_W = 64
EPS = 1e-6
ROPE_THETA = 10000.0
MLA_HEADS = 8
MLA_Q_RANK = 256
MLA_KV_RANK = 128
MLA_NOPE = 64
MLA_ROPE = 32
MLA_V = 64
MLA_QK = MLA_NOPE + MLA_ROPE
GQA_HEADS = 8
GQA_KV_HEADS = 2
GQA_DIM = 64
GQA_GROUP = GQA_HEADS // GQA_KV_HEADS
ROPE_HALF = 16
LANES = 128
LOG2E = math.log2(math.e)
MLA_QSCALE = LOG2E / math.sqrt(MLA_QK)
GQA_QSCALE = LOG2E / math.sqrt(GQA_DIM)
NEG_BIG = -1e30

C_QLAT = 0
C_KVLAT = C_QLAT + MLA_Q_RANK
C_KROPE = C_KVLAT + MLA_KV_RANK
C_Z = C_KROPE + LANES
C_QB = C_Z + 2 * 512
C_KB = C_QB + GQA_HEADS * LANES
C_VB = C_KB + GQA_KV_HEADS * LANES
N_PROJ = C_VB + 2 * LANES

VMEM_LIMIT = 56 * 1024 * 1024


def _cparams(sem):
    return pltpu.CompilerParams(dimension_semantics=sem, vmem_limit_bytes=VMEM_LIMIT)


def _ada_kernel(c_ref, w_ref, b_ref, o_ref):
    c = c_ref[...]
    sc = (c * jax.nn.sigmoid(c)).astype(jnp.bfloat16)
    o_ref[0] = jnp.dot(sc, w_ref[0], preferred_element_type=jnp.float32) + b_ref[0]


def _ada_call(c_pad, w_ada, b_ada):
    n = 3 * D_MODEL
    tn = 1024
    return pl.pallas_call(
        _ada_kernel,
        grid=(DEPTH, n // tn),
        in_specs=[pl.BlockSpec((8, D_MODEL), lambda l, j: (0, 0)),
                  pl.BlockSpec((1, D_MODEL, tn), lambda l, j: (l, 0, j)),
                  pl.BlockSpec((1, 1, tn), lambda l, j: (l, 0, j))],
        out_specs=pl.BlockSpec((1, 8, tn), lambda l, j: (l, 0, j)),
        out_shape=jax.ShapeDtypeStruct((DEPTH, 8, n), jnp.float32),
        compiler_params=_cparams(("arbitrary", "arbitrary")),
        name="adaln",
    )(c_pad, w_ada, b_ada)


def _rope(x, c, sa, sb):
    up = pltpu.roll(x, LANES - ROPE_HALF, axis=1)
    dn = pltpu.roll(x, ROPE_HALF, axis=1)
    return x * c + up * sa + dn * sb


def _proj_kernel(x_ref, mod_ref, gn_ref, win_ref, gqa_ref, wuq_ref, gkva_ref, wuk_ref, wuv_ref,
                 gqn_ref, gkn_ref, tab_ref,
                 qa_ref, ka_ref, va_ref, z_ref, qb_ref, kb_ref, vb_ref):
    x = x_ref[0]
    shift = mod_ref[0, 0:1, :]
    scale = mod_ref[0, 1:2, :]
    ms = jnp.mean(x * x, axis=-1, keepdims=True)
    h = (x * lax.rsqrt(ms + EPS)) * gn_ref[...]
    h = (h * (1.0 + scale) + shift).astype(jnp.bfloat16)

    def proj(lo, n):
        return jnp.dot(h, win_ref[:, lo:lo + n], preferred_element_type=jnp.float32)

    def tab(i):
        return tab_ref[:, i * LANES:(i + 1) * LANES]

    ql = proj(C_QLAT, MLA_Q_RANK)
    qn = (ql * lax.rsqrt(jnp.mean(ql * ql, axis=-1, keepdims=True) + EPS)) * gqa_ref[...]
    qa = jnp.dot(qn.astype(jnp.bfloat16), wuq_ref[...], preferred_element_type=jnp.float32)
    cq, saq, sbq = tab(0), tab(1), tab(2)
    for hh in range(MLA_HEADS):
        sl = slice(hh * LANES, (hh + 1) * LANES)
        qa_ref[0, :, sl] = _rope(qa[:, sl], cq, saq, sbq).astype(jnp.bfloat16)

    kvl = proj(C_KVLAT, MLA_KV_RANK)
    kvn = ((kvl * lax.rsqrt(jnp.mean(kvl * kvl, axis=-1, keepdims=True) + EPS))
           * gkva_ref[...]).astype(jnp.bfloat16)
    kpe = _rope(proj(C_KROPE, LANES), tab(3), tab(4), tab(5))
    kn = jnp.dot(kvn, wuk_ref[...], preferred_element_type=jnp.float32)
    for hh in range(MLA_HEADS):
        sl = slice(hh * LANES, (hh + 1) * LANES)
        ka_ref[0, :, sl] = (kn[:, sl] + kpe).astype(jnp.bfloat16)
    va_ref[0] = jnp.dot(kvn, wuv_ref[...], preferred_element_type=jnp.float32).astype(jnp.bfloat16)

    z = proj(C_Z, 1024)
    z_ref[0] = z * jax.nn.sigmoid(z)

    gqn = gqn_ref[...]
    gkn = gkn_ref[...]
    cbq, sabq, sbbq = tab(6), tab(7), tab(8)
    cbk, sabk, sbbk = tab(9), tab(10), tab(11)
    qb = proj(C_QB, GQA_HEADS * LANES)
    for hh in range(GQA_HEADS):
        sl = slice(hh * LANES, (hh + 1) * LANES)
        xh = qb[:, sl]
        r = lax.rsqrt(jnp.sum(xh * xh, axis=-1, keepdims=True) * (1.0 / GQA_DIM) + EPS)
        qb_ref[0, :, sl] = _rope(xh * r * gqn, cbq, sabq, sbbq).astype(jnp.bfloat16)
    kb = proj(C_KB, GQA_KV_HEADS * LANES)
    for hh in range(GQA_KV_HEADS):
        sl = slice(hh * LANES, (hh + 1) * LANES)
        xh = kb[:, sl]
        r = lax.rsqrt(jnp.sum(xh * xh, axis=-1, keepdims=True) * (1.0 / GQA_DIM) + EPS)
        kb_ref[0, :, sl] = _rope(xh * r * gkn, cbk, sabk, sbbk).astype(jnp.bfloat16)
    vb_ref[0] = proj(C_VB, 2 * LANES).astype(jnp.bfloat16)


def _proj_call(x, mod, lw, tabs, tm):
    B, S, D = x.shape
    full = lambda a: pl.BlockSpec(a.shape, lambda b, i: (0,) * a.ndim)
    tok = lambda n: pl.BlockSpec((1, tm, n), lambda b, i: (b, i, 0))
    outs = [("qa", 1024, jnp.bfloat16), ("ka", 1024, jnp.bfloat16), ("va", 512, jnp.bfloat16),
            ("z", 1024, jnp.float32), ("qb", 1024, jnp.bfloat16), ("kb", 256, jnp.bfloat16),
            ("vb", 256, jnp.bfloat16)]
    return pl.pallas_call(
        _proj_kernel,
        grid=(B, S // tm),
        in_specs=[tok(D),
                  pl.BlockSpec((1, 8, D), lambda b, i: (b, 0, 0)),
                  full(lw["g_norm"]), full(lw["w_in"]), full(lw["g_qa"]), full(lw["w_uq"]),
                  full(lw["g_kva"]), full(lw["w_uk"]), full(lw["w_uv"]),
                  full(lw["g_qn"]), full(lw["g_kn"]),
                  pl.BlockSpec((tm, 12 * LANES), lambda b, i: (i, 0))],
        out_specs=[tok(n) for _, n, _ in outs],
        out_shape=[jax.ShapeDtypeStruct((B, S, n), dt) for _, n, dt in outs],
        compiler_params=_cparams(("parallel", "parallel")),
        name="proj",
    )(x, mod, lw["g_norm"], lw["w_in"], lw["g_qa"], lw["w_uq"], lw["g_kva"], lw["w_uk"],
      lw["w_uv"], lw["g_qn"], lw["g_kn"], tabs)


def _softmax_step(s, m_prev, l_prev):
    m_new = jnp.maximum(m_prev, jnp.max(s, axis=-1, keepdims=True))
    alpha = jnp.exp2(m_prev - m_new)
    p = jnp.exp2(s - m_new)
    l_new = alpha * l_prev + jnp.sum(p, axis=-1, keepdims=True)
    return p.astype(jnp.bfloat16), alpha, m_new, l_new


def _nt_dot(a, b):
    return lax.dot_general(a, b, (((1,), (1,)), ((), ())), preferred_element_type=jnp.float32)


def _mla_kernel(q_ref, k_ref, v_ref, o_ref, m_sc, l_sc, acc_sc):
    ki = pl.program_id(3)

    @pl.when(ki == 0)
    def _():
        m_sc[...] = jnp.full(m_sc.shape, NEG_BIG, jnp.float32)
        l_sc[...] = jnp.zeros(l_sc.shape, jnp.float32)
        acc_sc[...] = jnp.zeros(acc_sc.shape, jnp.float32)

    v = v_ref[0]
    lane = lax.broadcasted_iota(jnp.int32, v.shape, 1)
    zero = jnp.zeros_like(v)
    vh = (jnp.where(lane < MLA_V, v, zero), jnp.where(lane >= MLA_V, v, zero))
    pv = None
    alphas = []
    for hh in range(2):
        sl = slice(hh * LANES, (hh + 1) * LANES)
        s = _nt_dot(q_ref[0, :, sl], k_ref[0, :, sl])
        p, alpha, m_new, l_new = _softmax_step(s, m_sc[hh], l_sc[hh])
        m_sc[hh] = m_new
        l_sc[hh] = l_new
        alphas.append(alpha)
        d = jnp.dot(p, vh[hh], preferred_element_type=jnp.float32)
        pv = d if pv is None else pv + d
    lane_o = lax.broadcasted_iota(jnp.int32, acc_sc.shape, 1)
    alpha_pair = jnp.where(lane_o < MLA_V, alphas[0], alphas[1])
    acc_sc[...] = acc_sc[...] * alpha_pair + pv

    @pl.when(ki == pl.num_programs(3) - 1)
    def _():
        l_pair = jnp.where(lane_o < MLA_V, l_sc[0], l_sc[1])
        o_ref[0] = acc_sc[...] / l_pair


def _mla_call(qa, ka, va, tq, tk):
    B, S, _ = qa.shape
    return pl.pallas_call(
        _mla_kernel,
        grid=(B, MLA_HEADS // 2, S // tq, S // tk),
        in_specs=[pl.BlockSpec((1, tq, 2 * LANES), lambda b, h, i, j: (b, i, h)),
                  pl.BlockSpec((1, tk, 2 * LANES), lambda b, h, i, j: (b, j, h)),
                  pl.BlockSpec((1, tk, LANES), lambda b, h, i, j: (b, j, h))],
        out_specs=pl.BlockSpec((1, tq, LANES), lambda b, h, i, j: (b, i, h)),
        out_shape=jax.ShapeDtypeStruct((B, S, MLA_HEADS * MLA_V), jnp.float32),
        scratch_shapes=[pltpu.VMEM((2, tq, 1), jnp.float32),
                        pltpu.VMEM((2, tq, 1), jnp.float32),
                        pltpu.VMEM((tq, LANES), jnp.float32)],
        compiler_params=_cparams(("parallel", "parallel", "parallel", "arbitrary")),
        name="mla_attn",
    )(qa, ka, va)


def _gqa_kernel(q_ref, k_ref, v_ref, o_ref, q_sc, m_sc, l_sc, acc_sc):
    ki = pl.program_id(3)
    tq = q_ref.shape[1]

    @pl.when(ki == 0)
    def _():
        for g in range(GQA_GROUP):
            q_sc[g * tq:(g + 1) * tq, :] = q_ref[0, :, g * LANES:(g + 1) * LANES]
        m_sc[...] = jnp.full(m_sc.shape, NEG_BIG, jnp.float32)
        l_sc[...] = jnp.zeros(l_sc.shape, jnp.float32)
        acc_sc[...] = jnp.zeros(acc_sc.shape, jnp.float32)

    v = v_ref[0]
    lane = lax.broadcasted_iota(jnp.int32, v.shape, 1)
    zero = jnp.zeros_like(v)
    vh = (jnp.where(lane < GQA_DIM, v, zero), jnp.where(lane >= GQA_DIM, v, zero))
    s = _nt_dot(q_sc[...], k_ref[0])
    p, alpha, m_new, l_new = _softmax_step(s, m_sc[...], l_sc[...])
    m_sc[...] = m_new
    l_sc[...] = l_new
    lane_o = lax.broadcasted_iota(jnp.int32, (tq, LANES), 1)
    for pr in range(GQA_GROUP // 2):
        r0 = slice((2 * pr) * tq, (2 * pr + 1) * tq)
        r1 = slice((2 * pr + 1) * tq, (2 * pr + 2) * tq)
        pv = (jnp.dot(p[r0], vh[0], preferred_element_type=jnp.float32)
              + jnp.dot(p[r1], vh[1], preferred_element_type=jnp.float32))
        alpha_pair = jnp.where(lane_o < GQA_DIM, alpha[r0], alpha[r1])
        acc_sc[pr] = acc_sc[pr] * alpha_pair + pv

    @pl.when(ki == pl.num_programs(3) - 1)
    def _():
        for pr in range(GQA_GROUP // 2):
            r0 = slice((2 * pr) * tq, (2 * pr + 1) * tq)
            r1 = slice((2 * pr + 1) * tq, (2 * pr + 2) * tq)
            l_pair = jnp.where(lane_o < GQA_DIM, l_sc[r0], l_sc[r1])
            o_ref[0, :, pr * LANES:(pr + 1) * LANES] = acc_sc[pr] / l_pair


def _gqa_call(qb, kb, vb, tq, tk):
    B, S, _ = qb.shape
    return pl.pallas_call(
        _gqa_kernel,
        grid=(B, GQA_KV_HEADS, S // tq, S // tk),
        in_specs=[pl.BlockSpec((1, tq, GQA_GROUP * LANES), lambda b, g, i, j: (b, i, g)),
                  pl.BlockSpec((1, tk, LANES), lambda b, g, i, j: (b, j, g)),
                  pl.BlockSpec((1, tk, LANES), lambda b, g, i, j: (b, j, g))],
        out_specs=pl.BlockSpec((1, tq, GQA_GROUP * GQA_DIM), lambda b, g, i, j: (b, i, g)),
        out_shape=jax.ShapeDtypeStruct((B, S, GQA_HEADS * GQA_DIM), jnp.float32),
        scratch_shapes=[pltpu.VMEM((GQA_GROUP * tq, LANES), jnp.bfloat16),
                        pltpu.VMEM((GQA_GROUP * tq, 1), jnp.float32),
                        pltpu.VMEM((GQA_GROUP * tq, 1), jnp.float32),
                        pltpu.VMEM((GQA_GROUP // 2, tq, LANES), jnp.float32)],
        compiler_params=_cparams(("parallel", "parallel", "parallel", "arbitrary")),
        name="gqa_attn",
    )(qb, kb, vb)


def _out_kernel(x_ref, oa_ref, ob_ref, z_ref, mod_ref, w_ref, y_ref):
    ya = (oa_ref[0] * z_ref[0, :, 0:512]).astype(jnp.bfloat16)
    yb = (ob_ref[0] * z_ref[0, :, 512:1024]).astype(jnp.bfloat16)
    out = (jnp.dot(ya, w_ref[0:512, :], preferred_element_type=jnp.float32)
           + jnp.dot(yb, w_ref[512:1024, :], preferred_element_type=jnp.float32))
    y_ref[0] = x_ref[0] + mod_ref[0, 2:3, :] * out


def _out_call(x, oa, ob, z, mod, w_out, tm):
    B, S, D = x.shape
    tok = lambda n: pl.BlockSpec((1, tm, n), lambda b, i: (b, i, 0))
    return pl.pallas_call(
        _out_kernel,
        grid=(B, S // tm),
        in_specs=[tok(D), tok(512), tok(512), tok(1024),
                  pl.BlockSpec((1, 8, D), lambda b, i: (b, 0, 0)),
                  pl.BlockSpec(w_out.shape, lambda b, i: (0, 0))],
        out_specs=tok(D),
        out_shape=jax.ShapeDtypeStruct((B, S, D), jnp.float32),
        compiler_params=_cparams(("parallel", "parallel")),
        name="out_proj",
    )(x, oa, ob, z, mod, w_out)


def _final_kernel(x_ref, g_ref, y_ref):
    x = x_ref[0]
    y_ref[0] = (x * lax.rsqrt(jnp.mean(x * x, axis=-1, keepdims=True) + EPS)) * g_ref[...]


def _final_call(x, g, tm):
    B, S, D = x.shape
    tok = pl.BlockSpec((1, tm, D), lambda b, i: (b, i, 0))
    return pl.pallas_call(
        _final_kernel,
        grid=(B, S // tm),
        in_specs=[tok, pl.BlockSpec((1, D), lambda b, i: (0, 0))],
        out_specs=tok,
        out_shape=jax.ShapeDtypeStruct((B, S, D), jnp.float32),
        compiler_params=_cparams(("parallel", "parallel")),
        name="final_norm",
    )(x, g)


def _rope_tables(S):
    inv = ROPE_THETA ** (-jnp.arange(ROPE_HALF, dtype=jnp.float32) / ROPE_HALF)
    t = jnp.arange(S, dtype=jnp.int32)
    row = (t // GRID_W).astype(jnp.float32)
    col = (t % GRID_W).astype(jnp.float32)
    tf = t.astype(jnp.float32)

    def cs(pos):
        ang = pos[:, None] * inv[None, :]
        return jnp.cos(ang), jnp.sin(ang)

    ones = lambda n: jnp.ones((S, n), jnp.float32)
    zeros = lambda n: jnp.zeros((S, n), jnp.float32)
    ct, st = cs(tf)
    cr, sr = cs(row)
    cc, sc = cs(col)
    mla = (jnp.concatenate([ones(64), ct, ct, ones(32)], 1),
           jnp.concatenate([zeros(64), -st, zeros(16), zeros(32)], 1),
           jnp.concatenate([zeros(64), zeros(16), st, zeros(32)], 1))
    ax = (jnp.concatenate([cr, cr, cc, cc, ones(64)], 1),
          jnp.concatenate([-sr, zeros(16), -sc, zeros(16), zeros(64)], 1),
          jnp.concatenate([zeros(16), sr, zeros(16), sc, zeros(64)], 1))
    parts = ([a * MLA_QSCALE for a in mla] + list(mla)
             + [a * GQA_QSCALE for a in ax] + list(ax))
    return jnp.concatenate(parts, axis=1)


def _np_in_perm():
    zc = 2208
    idx = np.full((N_PROJ,), zc, np.int32)
    idx[C_QLAT:C_QLAT + 256] = np.arange(0, 256)
    idx[C_KVLAT:C_KVLAT + 128] = np.arange(256, 384)
    idx[C_KROPE + 64:C_KROPE + 96] = np.arange(384, 416)
    idx[C_Z:C_Z + 512] = np.arange(416, 928)
    idx[C_Z + 512:C_Z + 1024] = np.arange(1696, 2208)
    for h in range(GQA_HEADS):
        idx[C_QB + h * LANES:C_QB + h * LANES + 64] = np.arange(928 + 64 * h, 928 + 64 * h + 64)
    for h in range(GQA_KV_HEADS):
        idx[C_KB + h * LANES:C_KB + h * LANES + 64] = np.arange(1440 + 64 * h, 1440 + 64 * h + 64)
        for r in range(2):
            lo = C_VB + h * LANES + 64 * r
            idx[lo:lo + 64] = np.arange(1568 + 64 * h, 1568 + 64 * h + 64)
    return idx


def _np_uq_perm():
    zc = MLA_HEADS * MLA_QK
    idx = np.full((MLA_HEADS * LANES,), zc, np.int32)
    for h in range(MLA_HEADS):
        idx[h * LANES:h * LANES + MLA_QK] = np.arange(h * MLA_QK, (h + 1) * MLA_QK)
    return idx


def _np_ukv_perm():
    zc = MLA_HEADS * (MLA_NOPE + MLA_V)
    ik = np.full((MLA_HEADS * LANES,), zc, np.int32)
    iv = np.zeros((MLA_HEADS * MLA_V,), np.int32)
    for h in range(MLA_HEADS):
        ik[h * LANES:h * LANES + 64] = np.arange(h * 128, h * 128 + 64)
        iv[h * 64:(h + 1) * 64] = np.arange(h * 128 + 64, h * 128 + 128)
    return ik, iv


def _take_cols(w, idx):
    wz = jnp.concatenate([w, jnp.zeros((w.shape[0], 1), w.dtype)], axis=1)
    return jnp.take(wz, jnp.asarray(idx), axis=1)


def _layer_weights(l, w_in, g_norm, g_qa, w_uq, g_kva, w_ukv, g_qn, g_kn, w_out):
    bf = jnp.bfloat16
    ik, iv = _np_ukv_perm()
    pad64 = lambda g: jnp.concatenate([g, jnp.zeros((LANES - GQA_DIM,), g.dtype)])[None, :]
    return {
        "g_norm": g_norm[l][None, :],
        "w_in": _take_cols(w_in[l], _np_in_perm()).astype(bf),
        "g_qa": g_qa[l][None, :],
        "w_uq": _take_cols(w_uq[l], _np_uq_perm()).astype(bf),
        "g_kva": g_kva[l][None, :],
        "w_uk": _take_cols(w_ukv[l], ik).astype(bf),
        "w_uv": _take_cols(w_ukv[l], iv).astype(bf),
        "g_qn": pad64(g_qn[l]),
        "g_kn": pad64(g_kn[l]),
        "w_out": w_out[l].astype(bf),
    }


def _trunk(x, mods, lws, g_final, tm, tq_a, tq_b, tk):
    B, S, D = x.shape
    tabs = _rope_tables(S)
    for l in range(DEPTH):
        mod = mods[l]
        lw = lws[l]
        qa, ka, va, z, qb, kb, vb = _proj_call(x, mod, lw, tabs, tm)
        oa = _mla_call(qa, ka, va, tq_a, tk)
        ob = _gqa_call(qb, kb, vb, tq_b, tk)
        x = _out_call(x, oa, ob, z, mod, lw["w_out"], tm)
    return _final_call(x, g_final[None, :], tm)


def kernel(x_prompt, x_sample, c_prompt, c_sample, w_ada, b_ada, g_norm, w_in, g_qa, w_uq,
           g_kva, w_ukv, g_qn, g_kn, w_out, g_final):
    nb_p, nb_s = c_prompt.shape[0], c_sample.shape[0]
    c_pad = jnp.concatenate(
        [c_prompt, c_sample, jnp.zeros((8 - nb_p - nb_s, D_MODEL), jnp.float32)], axis=0)
    mod_all = _ada_call(c_pad, w_ada.astype(jnp.bfloat16), b_ada[:, None, :])

    def mods_for(lo, n):
        out = []
        for l in range(DEPTH):
            m = mod_all[l, lo:lo + n].reshape(n, 3, D_MODEL)
            out.append(jnp.concatenate([m, jnp.zeros((n, 5, D_MODEL), jnp.float32)], axis=1))
        return out

    lws = [_layer_weights(l, w_in, g_norm, g_qa, w_uq, g_kva, w_ukv, g_qn, g_kn, w_out)
           for l in range(DEPTH)]
    y_p = _trunk(x_prompt, mods_for(0, nb_p), lws, g_final, 256, 512, 256, 512)
    y_s = _trunk(x_sample, mods_for(nb_p, nb_s), lws, g_final, 256, 512, 256, 512)
    return (y_p, y_s)
```

```python
import functools
import math

import numpy as np
import jax
import jax.numpy as jnp
from jax import lax
from jax.experimental import pallas as pl
from jax.experimental.pallas import tpu as pltpu

D_MODEL = 1024
DEPTH = 4
GRID_W = 64
EPS = 1e-6
ROPE_THETA = 10000.0
MLA_HEADS = 8
MLA_Q_RANK = 256
MLA_KV_RANK = 128
MLA_NOPE = 64
MLA_ROPE = 32
MLA_V = 64
MLA_QK = MLA_NOPE + MLA_ROPE
GQA_HEADS = 8
GQA_KV_HEADS = 2
GQA_DIM = 64
GQA_GROUP = GQA_HEADS // GQA_KV_HEADS
HEAD_V = 64
ROPE_HALF = 16
LANES = 128
MXU_N = 256
LOG2E = math.log2(math.e)
MLA_QSCALE = LOG2E / math.sqrt(MLA_QK)
GQA_QSCALE = LOG2E / math.sqrt(GQA_DIM)
NEG_BIG = -1e30

F_QLAT = 0
F_KVLAT = F_QLAT + MLA_Q_RANK
F_Z = F_KVLAT + MLA_KV_RANK
F_QB = F_Z + 1024
F_VB = F_QB + GQA_HEADS * GQA_DIM
N_F = F_VB + GQA_KV_HEADS * GQA_DIM
T_KVLAT = 0
T_KROPE = T_KVLAT + MLA_KV_RANK
T_KB = T_KROPE + LANES
N_T = T_KB + GQA_KV_HEADS * LANES
R_MLA_C, R_MLA_S, R_ROW_C, R_ROW_S, R_COL_C, R_COL_S = (16 * i for i in range(6))
N_TABF = 96

VMEM_LIMIT = 56 * 1024 * 1024
TOKEN_TILE = 512
LOOKAHEAD = 3
CHUNK_UNROLL = 4


def _cparams(sem):
    return pltpu.CompilerParams(dimension_semantics=sem, vmem_limit_bytes=VMEM_LIMIT)


def _nt_dot(a, b):
    return lax.dot_general(a, b, (((1,), (1,)), ((), ())), preferred_element_type=jnp.float32)


def _tn_dot(a, b):
    return lax.dot_general(a, b, (((0,), (0,)), ((), ())), preferred_element_type=jnp.float32)


def _ada_kernel(c_ref, w_ref, b_ref, o_ref):
    c = c_ref[...]
    sc = (c * jax.nn.sigmoid(c)).astype(jnp.bfloat16)
    o_ref[0] = jnp.dot(sc, w_ref[0], preferred_element_type=jnp.float32) + b_ref[0]


def _ada_call(c_pad, w_ada, b_ada):
    n = 3 * D_MODEL
    tn = 1024
    return pl.pallas_call(
        _ada_kernel,
        grid=(DEPTH, n // tn),
        in_specs=[pl.BlockSpec((8, D_MODEL), lambda l, j: (0, 0)),
                  pl.BlockSpec((1, D_MODEL, tn), lambda l, j: (l, 0, j)),
                  pl.BlockSpec((1, 1, tn), lambda l, j: (l, 0, j))],
        out_specs=pl.BlockSpec((1, 8, tn), lambda l, j: (l, 0, j)),
        out_shape=jax.ShapeDtypeStruct((DEPTH, 8, n), jnp.float32),
        compiler_params=_cparams(("arbitrary", "arbitrary")),
        name="adaln",
    )(c_pad, w_ada, b_ada)


def _rope_lanes(x, c, sa, sb):
    up = pltpu.roll(x, LANES - ROPE_HALF, axis=1)
    dn = pltpu.roll(x, ROPE_HALF, axis=1)
    return x * c + up * sa + dn * sb


def _proj_kernel(x_ref, mod_ref, gn_ref, wf_ref, wt_ref, gqa_ref, wuq_ref, gkva_t_ref, gkva_f_ref,
                 wuk_ref, wuv_ref, gqn_ref, gkn_ref, tabf_ref, tabt_ref,
                 qa_ref, ka_ref, va_ref, z_ref, qb_ref, kb_ref, vb_ref):
    bf = jnp.bfloat16
    x = x_ref[0]
    shift = mod_ref[0, 0:1, :]
    scale = mod_ref[0, 1:2, :]
    ms = jnp.mean(x * x, axis=-1, keepdims=True)
    h = (x * lax.rsqrt(ms + EPS)) * gn_ref[...]
    h = (h * (1.0 + scale) + shift).astype(bf)

    def proj_f(lo, n):
        return _nt_dot(wf_ref[lo:lo + n, :], h)

    def proj_t(lo, n):
        return jnp.dot(h, wt_ref[:, lo:lo + n], preferred_element_type=jnp.float32)

    def tabf(lo):
        return tabf_ref[lo:lo + ROPE_HALF, :]

    def tabt(i):
        return tabt_ref[:, i * LANES:(i + 1) * LANES]

    ql = proj_f(F_QLAT, MLA_Q_RANK)
    qn = (ql * lax.rsqrt(jnp.mean(ql * ql, axis=0, keepdims=True) + EPS)) * gqa_ref[...]
    qa = jnp.dot(wuq_ref[...], qn.astype(bf), preferred_element_type=jnp.float32)
    cm, sm = tabf(R_MLA_C), tabf(R_MLA_S)
    zpad = jnp.zeros((LANES - MLA_QK, x.shape[0]), bf)
    for hh in range(MLA_HEADS):
        src = hh * MLA_QK
        dst = hh * LANES
        qa_ref[0, dst:dst + MLA_NOPE, :] = (qa[src:src + MLA_NOPE] * MLA_QSCALE).astype(bf)
        x1 = qa[src + MLA_NOPE:src + MLA_NOPE + ROPE_HALF]
        x2 = qa[src + MLA_NOPE + ROPE_HALF:src + MLA_QK]
        qa_ref[0, dst + MLA_NOPE:dst + MLA_NOPE + ROPE_HALF, :] = (x1 * cm - x2 * sm).astype(bf)
        qa_ref[0, dst + MLA_NOPE + ROPE_HALF:dst + MLA_QK, :] = (x1 * sm + x2 * cm).astype(bf)
        qa_ref[0, dst + MLA_QK:dst + LANES, :] = zpad

    kvl_f = proj_f(F_KVLAT, MLA_KV_RANK)
    kvn_f = ((kvl_f * lax.rsqrt(jnp.mean(kvl_f * kvl_f, axis=0, keepdims=True) + EPS))
             * gkva_f_ref[...]).astype(bf)
    va_ref[0, 0] = jnp.dot(wuv_ref[...], kvn_f, preferred_element_type=jnp.float32).astype(bf)

    kvl_t = proj_t(T_KVLAT, MLA_KV_RANK)
    kvn_t = ((kvl_t * lax.rsqrt(jnp.mean(kvl_t * kvl_t, axis=-1, keepdims=True) + EPS))
             * gkva_t_ref[...]).astype(bf)
    kpe = _rope_lanes(proj_t(T_KROPE, LANES), tabt(0), tabt(1), tabt(2))
    kn = jnp.dot(kvn_t, wuk_ref[...], preferred_element_type=jnp.float32)
    for hh in range(MLA_HEADS):
        sl = slice(hh * LANES, (hh + 1) * LANES)
        ka_ref[0, :, sl] = (kn[:, sl] + kpe).astype(bf)

    z = proj_f(F_Z, 1024)
    z_ref[0] = z * jax.nn.sigmoid(z)

    gqn = gqn_ref[...]
    cr, sr, cc, sc = tabf(R_ROW_C), tabf(R_ROW_S), tabf(R_COL_C), tabf(R_COL_S)
    qb = proj_f(F_QB, GQA_HEADS * GQA_DIM)
    zpad = jnp.zeros((LANES - GQA_DIM, x.shape[0]), bf)
    for hh in range(GQA_HEADS):
        xh = qb[hh * GQA_DIM:(hh + 1) * GQA_DIM]
        r = lax.rsqrt(jnp.mean(xh * xh, axis=0, keepdims=True) + EPS)
        xn = xh * r * gqn
        dst = hh * LANES
        for j, (c, s) in enumerate(((cr, sr), (cc, sc))):
            x1 = xn[32 * j:32 * j + ROPE_HALF]
            x2 = xn[32 * j + ROPE_HALF:32 * j + 32]
            qb_ref[0, dst + 32 * j:dst + 32 * j + ROPE_HALF, :] = (x1 * c - x2 * s).astype(bf)
            qb_ref[0, dst + 32 * j + ROPE_HALF:dst + 32 * j + 32, :] = (x1 * s + x2 * c).astype(bf)
        qb_ref[0, dst + GQA_DIM:dst + LANES, :] = zpad

    gkn = gkn_ref[...]
    kb = proj_t(T_KB, GQA_KV_HEADS * LANES)
    for hh in range(GQA_KV_HEADS):
        sl = slice(hh * LANES, (hh + 1) * LANES)
        xh = kb[:, sl]
        r = lax.rsqrt(jnp.sum(xh * xh, axis=-1, keepdims=True) * (1.0 / GQA_DIM) + EPS)
        kb_ref[0, :, sl] = _rope_lanes(xh * r * gkn, tabt(3), tabt(4), tabt(5)).astype(bf)
    vb_ref[0, 0] = proj_f(F_VB, GQA_KV_HEADS * GQA_DIM).astype(bf)


def _proj_call(x, mod, lw, tabf, tabt):
    B, S, D = x.shape
    tm = TOKEN_TILE
    nt = S // tm
    bf = jnp.bfloat16
    full = lambda a: pl.BlockSpec(a.shape, lambda b, i: (0,) * a.ndim)
    tok = lambda n: pl.BlockSpec((1, tm, n), lambda b, i: (b, i, 0))
    feat = lambda n: pl.BlockSpec((1, n, tm), lambda b, i: (b, 0, i))
    chunked = lambda n: pl.BlockSpec((1, 1, n, tm), lambda b, i: (b, i, 0, 0))
    names = ["g_norm", "w_f", "w_t", "g_qa", "w_uq", "g_kva_t", "g_kva_f", "w_uk", "w_uv",
             "g_qn_f", "g_kn_t"]
    return pl.pallas_call(
        _proj_kernel,
        grid=(B, nt),
        in_specs=[tok(D), pl.BlockSpec((1, 8, D), lambda b, i: (b, 0, 0))]
                 + [full(lw[n]) for n in names]
                 + [pl.BlockSpec((N_TABF, tm), lambda b, i: (0, i)),
                    pl.BlockSpec((tm, 6 * LANES), lambda b, i: (i, 0))],
        out_specs=[feat(MLA_HEADS * LANES), tok(MLA_HEADS * LANES), chunked(MLA_HEADS * MLA_V),
                   feat(1024), feat(GQA_HEADS * LANES), tok(GQA_KV_HEADS * LANES),
                   chunked(GQA_KV_HEADS * GQA_DIM)],
        out_shape=[jax.ShapeDtypeStruct((B, MLA_HEADS * LANES, S), bf),
                   jax.ShapeDtypeStruct((B, S, MLA_HEADS * LANES), bf),
                   jax.ShapeDtypeStruct((B, nt, MLA_HEADS * MLA_V, tm), bf),
                   jax.ShapeDtypeStruct((B, 1024, S), jnp.float32),
                   jax.ShapeDtypeStruct((B, GQA_HEADS * LANES, S), bf),
                   jax.ShapeDtypeStruct((B, S, GQA_KV_HEADS * LANES), bf),
                   jax.ShapeDtypeStruct((B, nt, GQA_KV_HEADS * GQA_DIM, tm), bf)],
        compiler_params=_cparams(("parallel", "parallel")),
        name="proj",
    )(x, mod, *[lw[n] for n in names], tabf, tabt)


def _attn_kernel(q_ref, k_ref, v_ref, o_ref, m_sc, l_sc, acc_sc, s_sc, cm_sc, *, group):
    tq = q_ref.shape[2]
    tk = v_ref.shape[3]
    n_chunks = v_ref.shape[1]
    n_col = tq // MXU_N
    n_stream = group * n_col

    m_sc[...] = jnp.full(m_sc.shape, NEG_BIG, jnp.float32)
    l_sc[...] = jnp.zeros(l_sc.shape, jnp.float32)
    acc_sc[...] = jnp.zeros(acc_sc.shape, jnp.float32)

    n_items = CHUNK_UNROLL * n_stream

    def scores(c0, i):
        u, st = divmod(i, n_stream)
        g, j = divmod(st, n_col)
        c = jnp.minimum(c0 + u, n_chunks - 1)
        k_c = k_ref[0, pl.ds(pl.multiple_of(c * tk, tk), tk), :]
        q_t = q_ref[0, g * LANES:(g + 1) * LANES, j * MXU_N:(j + 1) * MXU_N]
        s = jnp.dot(k_c, q_t, preferred_element_type=jnp.float32)
        s_sc[i] = s
        cm_sc[i] = jnp.max(s, axis=0, keepdims=True)

    def update(c0, i):
        u, st = divmod(i, n_stream)
        m_prev = m_sc[st]
        m_new = jnp.maximum(m_prev, cm_sc[i])
        alpha = jnp.exp2(m_prev - m_new)
        p = jnp.exp2(s_sc[i] - m_new)
        l_sc[st] = alpha * l_sc[st] + jnp.sum(p, axis=0, keepdims=True)
        m_sc[st] = m_new
        pv = jnp.dot(v_ref[0, c0 + u], p.astype(jnp.bfloat16),
                     preferred_element_type=jnp.float32)
        acc_sc[st] = acc_sc[st] * alpha + pv

    for i in range(LOOKAHEAD):
        scores(0, i)

    def step(it, carry):
        c0 = it * CHUNK_UNROLL
        for i in range(n_items):
            nx = i + LOOKAHEAD
            if nx < n_items:
                scores(c0, nx)
            else:
                scores(c0 + CHUNK_UNROLL, nx - n_items)
            update(c0, i)
        return carry

    lax.fori_loop(0, n_chunks // CHUNK_UNROLL, step, 0)

    for st in range(n_stream):
        g, j = divmod(st, n_col)
        o_ref[0, g * HEAD_V:(g + 1) * HEAD_V, j * MXU_N:(j + 1) * MXU_N] = acc_sc[st] / l_sc[st]


def _attn_call(q_t, k, v_t, group, tq, name):
    B, _, S = q_t.shape
    n_kv = k.shape[2] // LANES
    n_chunks, tk = v_t.shape[1], v_t.shape[3]
    n_stream = group * (tq // MXU_N)
    return pl.pallas_call(
        functools.partial(_attn_kernel, group=group),
        grid=(B, n_kv, S // tq),
        in_specs=[pl.BlockSpec((1, group * LANES, tq), lambda b, h, i: (b, h, i)),
                  pl.BlockSpec((1, S, LANES), lambda b, h, i: (b, 0, h)),
                  pl.BlockSpec((1, n_chunks, HEAD_V, tk), lambda b, h, i: (b, 0, h, 0))],
        out_specs=pl.BlockSpec((1, group * HEAD_V, tq), lambda b, h, i: (b, h, i)),
        out_shape=jax.ShapeDtypeStruct((B, n_kv * group * HEAD_V, S), jnp.float32),
        scratch_shapes=[pltpu.VMEM((n_stream, 1, MXU_N), jnp.float32),
                        pltpu.VMEM((n_stream, 1, MXU_N), jnp.float32),
                        pltpu.VMEM((n_stream, HEAD_V, MXU_N), jnp.float32),
                        pltpu.VMEM((CHUNK_UNROLL * n_stream, tk, MXU_N), jnp.float32),
                        pltpu.VMEM((CHUNK_UNROLL * n_stream, 1, MXU_N), jnp.float32)],
        compiler_params=_cparams(("parallel", "parallel", "parallel")),
        name=name,
    )(q_t, k, v_t)


def _out_kernel(x_ref, oa_ref, ob_ref, z_ref, mod_ref, w_ref, y_ref):
    ya = (oa_ref[0] * z_ref[0, 0:512, :]).astype(jnp.bfloat16)
    yb = (ob_ref[0] * z_ref[0, 512:1024, :]).astype(jnp.bfloat16)
    out = _tn_dot(ya, w_ref[0:512, :]) + _tn_dot(yb, w_ref[512:1024, :])
    y_ref[0] = x_ref[0] + mod_ref[0, 2:3, :] * out


def _out_call(x, oa_t, ob_t, z_t, mod, w_out):
    B, S, D = x.shape
    tm = TOKEN_TILE
    tok = lambda n: pl.BlockSpec((1, tm, n), lambda b, i: (b, i, 0))
    feat = lambda n: pl.BlockSpec((1, n, tm), lambda b, i: (b, 0, i))
    return pl.pallas_call(
        _out_kernel,
        grid=(B, S // tm),
        in_specs=[tok(D), feat(512), feat(512), feat(1024),
                  pl.BlockSpec((1, 8, D), lambda b, i: (b, 0, 0)),
                  pl.BlockSpec(w_out.shape, lambda b, i: (0, 0))],
        out_specs=tok(D),
        out_shape=jax.ShapeDtypeStruct((B, S, D), jnp.float32),
        compiler_params=_cparams(("parallel", "parallel")),
        name="out_proj",
    )(x, oa_t, ob_t, z_t, mod, w_out)


def _final_kernel(x_ref, g_ref, y_ref):
    x = x_ref[0]
    y_ref[0] = (x * lax.rsqrt(jnp.mean(x * x, axis=-1, keepdims=True) + EPS)) * g_ref[...]


def _final_call(x, g):
    B, S, D = x.shape
    tm = TOKEN_TILE
    tok = pl.BlockSpec((1, tm, D), lambda b, i: (b, i, 0))
    return pl.pallas_call(
        _final_kernel,
        grid=(B, S // tm),
        in_specs=[tok, pl.BlockSpec((1, D), lambda b, i: (0, 0))],
        out_specs=tok,
        out_shape=jax.ShapeDtypeStruct((B, S, D), jnp.float32),
        compiler_params=_cparams(("parallel", "parallel")),
        name="final_norm",
    )(x, g)


def _rope_tables(S):
    inv = ROPE_THETA ** (-jnp.arange(ROPE_HALF, dtype=jnp.float32) / ROPE_HALF)
    t = jnp.arange(S, dtype=jnp.int32)
    row = (t // GRID_W).astype(jnp.float32)
    col = (t % GRID_W).astype(jnp.float32)
    tf = t.astype(jnp.float32)

    def cs(pos):
        ang = pos[:, None] * inv[None, :]
        return jnp.cos(ang), jnp.sin(ang)

    ones = lambda n: jnp.ones((S, n), jnp.float32)
    zeros = lambda n: jnp.zeros((S, n), jnp.float32)
    ct, st = cs(tf)
    cr, sr = cs(row)
    cc, sc = cs(col)
    tabf = jnp.concatenate([ct * MLA_QSCALE, st * MLA_QSCALE,
                            cr * GQA_QSCALE, sr * GQA_QSCALE,
                            cc * GQA_QSCALE, sc * GQA_QSCALE], axis=1).T
    mla = (jnp.concatenate([ones(64), ct, ct, ones(32)], 1),
           jnp.concatenate([zeros(64), -st, zeros(16), zeros(32)], 1),
           jnp.concatenate([zeros(64), zeros(16), st, zeros(32)], 1))
    ax = (jnp.concatenate([cr, cr, cc, cc, ones(64)], 1),
          jnp.concatenate([-sr, zeros(16), -sc, zeros(16), zeros(64)], 1),
          jnp.concatenate([zeros(16), sr, zeros(16), sc, zeros(64)], 1))
    tabt = jnp.concatenate(list(mla) + list(ax), axis=1)
    return tabf, tabt


def _np_in_perms():
    zc = 2208
    f = np.concatenate([np.arange(0, 256),
                        np.arange(256, 384),
                        np.arange(416, 928),
                        np.arange(1696, 2208),
                        np.arange(928, 1440),
                        np.arange(1568, 1696)]).astype(np.int32)
    t = np.full((N_T,), zc, np.int32)
    t[T_KVLAT:T_KVLAT + 128] = np.arange(256, 384)
    t[T_KROPE + 64:T_KROPE + 96] = np.arange(384, 416)
    for h in range(GQA_KV_HEADS):
        t[T_KB + h * LANES:T_KB + h * LANES + 64] = np.arange(1440 + 64 * h, 1440 + 64 * h + 64)
    return f, t


def _np_ukv_perm():
    zc = MLA_HEADS * (MLA_NOPE + MLA_V)
    ik = np.full((MLA_HEADS * LANES,), zc, np.int32)
    iv = np.zeros((MLA_HEADS * MLA_V,), np.int32)
    for h in range(MLA_HEADS):
        ik[h * LANES:h * LANES + 64] = np.arange(h * 128, h * 128 + 64)
        iv[h * 64:(h + 1) * 64] = np.arange(h * 128 + 64, h * 128 + 128)
    return ik, iv


def _take_cols(w, idx):
    wz = jnp.concatenate([w, jnp.zeros((w.shape[0], 1), w.dtype)], axis=1)
    return jnp.take(wz, jnp.asarray(idx), axis=1)


def _layer_weights(l, w_in, g_norm, g_qa, w_uq, g_kva, w_ukv, g_qn, g_kn, w_out):
    bf = jnp.bfloat16
    i_f, i_t = _np_in_perms()
    ik, iv = _np_ukv_perm()
    pad64 = lambda g: jnp.concatenate([g, jnp.zeros((LANES - GQA_DIM,), g.dtype)])[None, :]
    return {
        "g_norm": g_norm[l][None, :],
        "w_f": _take_cols(w_in[l], i_f).T.astype(bf),
        "w_t": _take_cols(w_in[l], i_t).astype(bf),
        "g_qa": g_qa[l][:, None],
        "w_uq": w_uq[l].T.astype(bf),
        "g_kva_t": g_kva[l][None, :],
        "g_kva_f": g_kva[l][:, None],
        "w_uk": _take_cols(w_ukv[l], ik).astype(bf),
        "w_uv": _take_cols(w_ukv[l], iv).T.astype(bf),
        "g_qn_f": g_qn[l][:, None],
        "g_kn_t": pad64(g_kn[l]),
        "w_out": w_out[l].astype(bf),
    }


def _trunk(x, mods, lws, g_final, tq_a, tq_b):
    B, S, D = x.shape
    tabf, tabt = _rope_tables(S)
    for l in range(DEPTH):
        mod = mods[l]
        lw = lws[l]
        qa_t, ka, va_t, z_t, qb_t, kb, vb_t = _proj_call(x, mod, lw, tabf, tabt)
        oa_t = _attn_call(qa_t, ka, va_t, 1, tq_a, "mla_attn")
        ob_t = _attn_call(qb_t, kb, vb_t, GQA_GROUP, tq_b, "gqa_attn")
        x = _out_call(x, oa_t, ob_t, z_t, mod, lw["w_out"])
    return _final_call(x, g_final[None, :])


def kernel(x_prompt, x_sample, c_prompt, c_sample, w_ada, b_ada, g_norm, w_in, g_qa, w_uq,
           g_kva, w_ukv, g_qn, g_kn, w_out, g_final):
    nb_p, nb_s = c_prompt.shape[0], c_sample.shape[0]
    c_pad = jnp.concatenate(
        [c_prompt, c_sample, jnp.zeros((8 - nb_p - nb_s, D_MODEL), jnp.float32)], axis=0)
    mod_all = _ada_call(c_pad, w_ada.astype(jnp.bfloat16), b_ada[:, None, :])

    def mods_for(lo, n):
        out = []
        for l in range(DEPTH):
            m = mod_all[l, lo:lo + n].reshape(n, 3, D_MODEL)
            out.append(jnp.concatenate([m, jnp.zeros((n, 5, D_MODEL), jnp.float32)], axis=1))
        return out

    lws = [_layer_weights(l, w_in, g_norm, g_qa, w_uq, g_kva, w_ukv, g_qn, g_kn, w_out)
           for l in range(DEPTH)]
    y_p = _trunk(x_prompt, mods_for(0, nb_p), lws, g_final, 1024, 256)
    y_s = _trunk(x_sample, mods_for(nb_p, nb_s), lws, g_final, 1024, 256)
    return (y_p, y_s)
```

```python
import functools
import math

import numpy as np
import jax
import jax.numpy as jnp
from jax import lax
from jax.experimental import pallas as pl
from jax.experimental.pallas import tpu as pltpu

D_MODEL = 1024
DEPTH = 4
GRID_W = 64
EPS = 1e-6
ROPE_THETA = 10000.0
MLA_HEADS = 8
MLA_Q_RANK = 256
MLA_KV_RANK = 128
MLA_NOPE = 64
MLA_ROPE = 32
MLA_V = 64
MLA_QK = MLA_NOPE + MLA_ROPE
GQA_HEADS = 8
GQA_KV_HEADS = 2
GQA_DIM = 64
GQA_GROUP = GQA_HEADS // GQA_KV_HEADS
HEAD_V = 64
ROPE_HALF = 16
LANES = 128
MXU_N = 256
LOG2E = math.log2(math.e)
MLA_QSCALE = LOG2E / math.sqrt(MLA_QK)
GQA_QSCALE = LOG2E / math.sqrt(GQA_DIM)
NEG_BIG = -1e30

F_QLAT = 0
F_KVLAT = F_QLAT + MLA_Q_RANK
F_Z = F_KVLAT + MLA_KV_RANK
F_QB = F_Z + 1024
F_VB = F_QB + GQA_HEADS * GQA_DIM
N_F = F_VB + GQA_KV_HEADS * GQA_DIM
T_KVLAT = 0
T_KROPE = T_KVLAT + MLA_KV_RANK
T_KB = T_KROPE + LANES
N_T = T_KB + GQA_KV_HEADS * LANES
R_MLA_C, R_MLA_S, R_ROW_C, R_ROW_S, R_COL_C, R_COL_S = (16 * i for i in range(6))
N_TABF = 96

VMEM_LIMIT = 56 * 1024 * 1024
TOKEN_TILE = 512
LOOKAHEAD = 3
SCORE_SLOTS = LOOKAHEAD + 1
HEAD_VX = 80
MAX_CHUNK_UNROLL = 4


def _cparams(sem):
    return pltpu.CompilerParams(dimension_semantics=sem, vmem_limit_bytes=VMEM_LIMIT)


def _nt_dot(a, b):
    return lax.dot_general(a, b, (((1,), (1,)), ((), ())), preferred_element_type=jnp.float32)


def _tn_dot(a, b):
    return lax.dot_general(a, b, (((0,), (0,)), ((), ())), preferred_element_type=jnp.float32)


def _ada_kernel(c_ref, w_ref, b_ref, o_ref):
    c = c_ref[...]
    sc = (c * jax.nn.sigmoid(c)).astype(jnp.bfloat16)
    o_ref[0] = jnp.dot(sc, w_ref[0], preferred_element_type=jnp.float32) + b_ref[0]


def _ada_call(c_pad, w_ada, b_ada):
    n = 3 * D_MODEL
    tn = 1024
    return pl.pallas_call(
        _ada_kernel,
        grid=(DEPTH, n // tn),
        in_specs=[pl.BlockSpec((8, D_MODEL), lambda l, j: (0, 0)),
                  pl.BlockSpec((1, D_MODEL, tn), lambda l, j: (l, 0, j)),
                  pl.BlockSpec((1, 1, tn), lambda l, j: (l, 0, j))],
        out_specs=pl.BlockSpec((1, 8, tn), lambda l, j: (l, 0, j)),
        out_shape=jax.ShapeDtypeStruct((DEPTH, 8, n), jnp.float32),
        compiler_params=_cparams(("arbitrary", "arbitrary")),
        name="adaln",
    )(c_pad, w_ada, b_ada)


def _rope_lanes(x, c, sa, sb):
    up = pltpu.roll(x, LANES - ROPE_HALF, axis=1)
    dn = pltpu.roll(x, ROPE_HALF, axis=1)
    return x * c + up * sa + dn * sb


def _proj_kernel(x_ref, mod_ref, gn_ref, wf_ref, wt_ref, gqa_ref, wuq_ref, gkva_t_ref, gkva_f_ref,
                 wuk_ref, wuv_ref, gqn_ref, gkn_ref, tabf_ref, tabt_ref,
                 qa_ref, ka_ref, va_ref, z_ref, qb_ref, kb_ref, vb_ref):
    bf = jnp.bfloat16
    x = x_ref[0]
    shift = mod_ref[0, 0:1, :]
    scale = mod_ref[0, 1:2, :]
    ms = jnp.mean(x * x, axis=-1, keepdims=True)
    h = (x * lax.rsqrt(ms + EPS)) * gn_ref[...]
    h = (h * (1.0 + scale) + shift).astype(bf)

    def proj_f(lo, n):
        return _nt_dot(wf_ref[lo:lo + n, :], h)

    def proj_t(lo, n):
        return jnp.dot(h, wt_ref[:, lo:lo + n], preferred_element_type=jnp.float32)

    def tabf(lo):
        return tabf_ref[lo:lo + ROPE_HALF, :]

    def tabt(i):
        return tabt_ref[:, i * LANES:(i + 1) * LANES]

    ql = proj_f(F_QLAT, MLA_Q_RANK)
    qn = (ql * lax.rsqrt(jnp.mean(ql * ql, axis=0, keepdims=True) + EPS)) * gqa_ref[...]
    qa = jnp.dot(wuq_ref[...], qn.astype(bf), preferred_element_type=jnp.float32)
    cm, sm = tabf(R_MLA_C), tabf(R_MLA_S)
    zpad = jnp.zeros((LANES - MLA_QK, x.shape[0]), bf)
    for hh in range(MLA_HEADS):
        src = hh * MLA_QK
        dst = hh * LANES
        qa_ref[0, dst:dst + MLA_NOPE, :] = (qa[src:src + MLA_NOPE] * MLA_QSCALE).astype(bf)
        x1 = qa[src + MLA_NOPE:src + MLA_NOPE + ROPE_HALF]
        x2 = qa[src + MLA_NOPE + ROPE_HALF:src + MLA_QK]
        qa_ref[0, dst + MLA_NOPE:dst + MLA_NOPE + ROPE_HALF, :] = (x1 * cm - x2 * sm).astype(bf)
        qa_ref[0, dst + MLA_NOPE + ROPE_HALF:dst + MLA_QK, :] = (x1 * sm + x2 * cm).astype(bf)
        qa_ref[0, dst + MLA_QK:dst + LANES, :] = zpad

    kvl_f = proj_f(F_KVLAT, MLA_KV_RANK)
    kvn_f = ((kvl_f * lax.rsqrt(jnp.mean(kvl_f * kvl_f, axis=0, keepdims=True) + EPS))
             * gkva_f_ref[...]).astype(bf)
    va = jnp.dot(wuv_ref[...], kvn_f, preferred_element_type=jnp.float32).astype(bf)
    row = lax.broadcasted_iota(jnp.int32, (HEAD_VX - HEAD_V, x.shape[0]), 0)
    ones_rows = jnp.where(row == 0, 1.0, 0.0).astype(bf)
    for hh in range(MLA_HEADS):
        va_ref[0, 0, hh * HEAD_VX:hh * HEAD_VX + HEAD_V, :] = va[hh * HEAD_V:(hh + 1) * HEAD_V]
        va_ref[0, 0, hh * HEAD_VX + HEAD_V:(hh + 1) * HEAD_VX, :] = ones_rows

    kvl_t = proj_t(T_KVLAT, MLA_KV_RANK)
    kvn_t = ((kvl_t * lax.rsqrt(jnp.mean(kvl_t * kvl_t, axis=-1, keepdims=True) + EPS))
             * gkva_t_ref[...]).astype(bf)
    kpe = _rope_lanes(proj_t(T_KROPE, LANES), tabt(0), tabt(1), tabt(2))
    kn = jnp.dot(kvn_t, wuk_ref[...], preferred_element_type=jnp.float32)
    for hh in range(MLA_HEADS):
        sl = slice(hh * LANES, (hh + 1) * LANES)
        ka_ref[0, :, sl] = (kn[:, sl] + kpe).astype(bf)

    z = proj_f(F_Z, 1024)
    z_ref[0] = z * jax.nn.sigmoid(z)

    gqn = gqn_ref[...]
    cr, sr, cc, sc = tabf(R_ROW_C), tabf(R_ROW_S), tabf(R_COL_C), tabf(R_COL_S)
    qb = proj_f(F_QB, GQA_HEADS * GQA_DIM)
    zpad = jnp.zeros((LANES - GQA_DIM, x.shape[0]), bf)
    for hh in range(GQA_HEADS):
        xh = qb[hh * GQA_DIM:(hh + 1) * GQA_DIM]
        r = lax.rsqrt(jnp.mean(xh * xh, axis=0, keepdims=True) + EPS)
        xn = xh * r * gqn
        dst = hh * LANES
        for j, (c, s) in enumerate(((cr, sr), (cc, sc))):
            x1 = xn[32 * j:32 * j + ROPE_HALF]
            x2 = xn[32 * j + ROPE_HALF:32 * j + 32]
            qb_ref[0, dst + 32 * j:dst + 32 * j + ROPE_HALF, :] = (x1 * c - x2 * s).astype(bf)
            qb_ref[0, dst + 32 * j + ROPE_HALF:dst + 32 * j + 32, :] = (x1 * s + x2 * c).astype(bf)
        qb_ref[0, dst + GQA_DIM:dst + LANES, :] = zpad

    gkn = gkn_ref[...]
    kb = proj_t(T_KB, GQA_KV_HEADS * LANES)
    for hh in range(GQA_KV_HEADS):
        sl = slice(hh * LANES, (hh + 1) * LANES)
        xh = kb[:, sl]
        r = lax.rsqrt(jnp.sum(xh * xh, axis=-1, keepdims=True) * (1.0 / GQA_DIM) + EPS)
        kb_ref[0, :, sl] = _rope_lanes(xh * r * gkn, tabt(3), tabt(4), tabt(5)).astype(bf)
    vb = proj_f(F_VB, GQA_KV_HEADS * GQA_DIM).astype(bf)
    for hh in range(GQA_KV_HEADS):
        vb_ref[0, 0, hh * HEAD_VX:hh * HEAD_VX + HEAD_V, :] = vb[hh * HEAD_V:(hh + 1) * HEAD_V]
        vb_ref[0, 0, hh * HEAD_VX + HEAD_V:(hh + 1) * HEAD_VX, :] = ones_rows


def _proj_call(x, mod, lw, tabf, tabt):
    B, S, D = x.shape
    tm = TOKEN_TILE
    nt = S // tm
    bf = jnp.bfloat16
    full = lambda a: pl.BlockSpec(a.shape, lambda b, i: (0,) * a.ndim)
    tok = lambda n: pl.BlockSpec((1, tm, n), lambda b, i: (b, i, 0))
    feat = lambda n: pl.BlockSpec((1, n, tm), lambda b, i: (b, 0, i))
    chunked = lambda n: pl.BlockSpec((1, 1, n, tm), lambda b, i: (b, i, 0, 0))
    names = ["g_norm", "w_f", "w_t", "g_qa", "w_uq", "g_kva_t", "g_kva_f", "w_uk", "w_uv",
             "g_qn_f", "g_kn_t"]
    return pl.pallas_call(
        _proj_kernel,
        grid=(B, nt),
        in_specs=[tok(D), pl.BlockSpec((1, 8, D), lambda b, i: (b, 0, 0))]
                 + [full(lw[n]) for n in names]
                 + [pl.BlockSpec((N_TABF, tm), lambda b, i: (0, i)),
                    pl.BlockSpec((tm, 6 * LANES), lambda b, i: (i, 0))],
        out_specs=[feat(MLA_HEADS * LANES), tok(MLA_HEADS * LANES), chunked(MLA_HEADS * HEAD_VX),
                   feat(1024), feat(GQA_HEADS * LANES), tok(GQA_KV_HEADS * LANES),
                   chunked(GQA_KV_HEADS * HEAD_VX)],
        out_shape=[jax.ShapeDtypeStruct((B, MLA_HEADS * LANES, S), bf),
                   jax.ShapeDtypeStruct((B, S, MLA_HEADS * LANES), bf),
                   jax.ShapeDtypeStruct((B, nt, MLA_HEADS * HEAD_VX, tm), bf),
                   jax.ShapeDtypeStruct((B, 1024, S), jnp.float32),
                   jax.ShapeDtypeStruct((B, GQA_HEADS * LANES, S), bf),
                   jax.ShapeDtypeStruct((B, S, GQA_KV_HEADS * LANES), bf),
                   jax.ShapeDtypeStruct((B, nt, GQA_KV_HEADS * HEAD_VX, tm), bf)],
        compiler_params=_cparams(("parallel", "parallel")),
        name="proj",
    )(x, mod, *[lw[n] for n in names], tabf, tabt)


def _attn_kernel(trips_ref, q_ref, k_ref, v_ref, o_ref, m_sc, acc_sc, s_sc, cm_sc, *,
                 group, unroll):
    tq = q_ref.shape[2]
    tk = v_ref.shape[3]
    n_chunks = v_ref.shape[1]
    n_col = tq // MXU_N
    n_stream = group * n_col
    n_items = unroll * n_stream
    n_iter = n_chunks // unroll

    m_sc[...] = jnp.full(m_sc.shape, NEG_BIG, jnp.float32)
    acc_sc[...] = jnp.zeros(acc_sc.shape, jnp.float32)

    def scores(c0, i):
        u, st = divmod(i, n_stream)
        g, j = divmod(st, n_col)
        k_c = k_ref[0, pl.ds(pl.multiple_of((c0 + u) * tk, tk), tk), :]
        q_t = q_ref[0, g * LANES:(g + 1) * LANES, j * MXU_N:(j + 1) * MXU_N]
        s = jnp.dot(k_c, q_t, preferred_element_type=jnp.float32)
        s_sc[i % SCORE_SLOTS] = s
        cm_sc[i % SCORE_SLOTS] = jnp.max(s, axis=0, keepdims=True)

    def update(c0, i):
        u, st = divmod(i, n_stream)
        m_prev = m_sc[st]
        m_new = jnp.maximum(m_prev, cm_sc[i % SCORE_SLOTS])
        alpha = jnp.exp2(m_prev - m_new)
        p = jnp.exp2(s_sc[i % SCORE_SLOTS] - m_new).astype(jnp.bfloat16)
        m_sc[st] = m_new
        pv = jnp.dot(v_ref[0, c0 + u], p, preferred_element_type=jnp.float32)
        acc_sc[st] = acc_sc[st] * alpha + pv

    def iteration(c0, last):
        for i in range(n_items):
            nx = i + LOOKAHEAD
            if nx < n_items:
                scores(c0, nx)
            elif not last:
                scores(c0 + unroll, nx - n_items)
            update(c0, i)

    for i in range(LOOKAHEAD):
        scores(0, i)
    if n_iter > 1:
        def step(it, carry):
            iteration(it * unroll, False)
            return carry
        lax.fori_loop(0, trips_ref[0], step, 0)
    iteration((n_iter - 1) * unroll, True)

    for st in range(n_stream):
        g, j = divmod(st, n_col)
        acc = acc_sc[st]
        o_ref[0, g * HEAD_V:(g + 1) * HEAD_V, j * MXU_N:(j + 1) * MXU_N] = (
            acc[0:HEAD_V] / acc[HEAD_V:HEAD_V + 1])


def _attn_call(q_t, k, v_t, group, tq, unroll, name):
    B, _, S = q_t.shape
    n_kv = k.shape[2] // LANES
    n_chunks, tk = v_t.shape[1], v_t.shape[3]
    n_stream = group * (tq // MXU_N)
    assert n_chunks % unroll == 0 and (unroll * n_stream) % SCORE_SLOTS == 0
    trips = jnp.full((1,), n_chunks // unroll - 1, jnp.int32)
    grid_spec = pltpu.PrefetchScalarGridSpec(
        num_scalar_prefetch=1,
        grid=(B, n_kv, S // tq),
        in_specs=[pl.BlockSpec((1, group * LANES, tq), lambda b, h, i, t: (b, h, i)),
                  pl.BlockSpec((1, S, LANES), lambda b, h, i, t: (b, 0, h)),
                  pl.BlockSpec((1, n_chunks, HEAD_VX, tk), lambda b, h, i, t: (b, 0, h, 0))],
        out_specs=pl.BlockSpec((1, group * HEAD_V, tq), lambda b, h, i, t: (b, h, i)),
        scratch_shapes=[pltpu.VMEM((n_stream, 1, MXU_N), jnp.float32),
                        pltpu.VMEM((n_stream, HEAD_VX, MXU_N), jnp.float32),
                        pltpu.VMEM((SCORE_SLOTS, tk, MXU_N), jnp.float32),
                        pltpu.VMEM((SCORE_SLOTS, 1, MXU_N), jnp.float32)])
    return pl.pallas_call(
        functools.partial(_attn_kernel, group=group, unroll=unroll),
        grid_spec=grid_spec,
        out_shape=jax.ShapeDtypeStruct((B, n_kv * group * HEAD_V, S), jnp.float32),
        compiler_params=_cparams(("parallel", "parallel", "parallel")),
        name=name,
    )(trips, q_t, k, v_t)


def _out_kernel(x_ref, oa_ref, ob_ref, z_ref, mod_ref, w_ref, y_ref):
    ya = (oa_ref[0] * z_ref[0, 0:512, :]).astype(jnp.bfloat16)
    yb = (ob_ref[0] * z_ref[0, 512:1024, :]).astype(jnp.bfloat16)
    out = _tn_dot(ya, w_ref[0:512, :]) + _tn_dot(yb, w_ref[512:1024, :])
    y_ref[0] = x_ref[0] + mod_ref[0, 2:3, :] * out


def _out_call(x, oa_t, ob_t, z_t, mod, w_out):
    B, S, D = x.shape
    tm = TOKEN_TILE
    tok = lambda n: pl.BlockSpec((1, tm, n), lambda b, i: (b, i, 0))
    feat = lambda n: pl.BlockSpec((1, n, tm), lambda b, i: (b, 0, i))
    return pl.pallas_call(
        _out_kernel,
        grid=(B, S // tm),
        in_specs=[tok(D), feat(512), feat(512), feat(1024),
                  pl.BlockSpec((1, 8, D), lambda b, i: (b, 0, 0)),
                  pl.BlockSpec(w_out.shape, lambda b, i: (0, 0))],
        out_specs=tok(D),
        out_shape=jax.ShapeDtypeStruct((B, S, D), jnp.float32),
        compiler_params=_cparams(("parallel", "parallel")),
        name="out_proj",
    )(x, oa_t, ob_t, z_t, mod, w_out)


def _final_kernel(x_ref, g_ref, y_ref):
    x = x_ref[0]
    y_ref[0] = (x * lax.rsqrt(jnp.mean(x * x, axis=-1, keepdims=True) + EPS)) * g_ref[...]


def _final_call(x, g):
    B, S, D = x.shape
    tm = TOKEN_TILE
    tok = pl.BlockSpec((1, tm, D), lambda b, i: (b, i, 0))
    return pl.pallas_call(
        _final_kernel,
        grid=(B, S // tm),
        in_specs=[tok, pl.BlockSpec((1, D), lambda b, i: (0, 0))],
        out_specs=tok,
        out_shape=jax.ShapeDtypeStruct((B, S, D), jnp.float32),
        compiler_params=_cparams(("parallel", "parallel")),
        name="final_norm",
    )(x, g)


def _rope_tables(S):
    inv = ROPE_THETA ** (-jnp.arange(ROPE_HALF, dtype=jnp.float32) / ROPE_HALF)
    t = jnp.arange(S, dtype=jnp.int32)
    row = (t // GRID_W).astype(jnp.float32)
    col = (t % GRID_W).astype(jnp.float32)
    tf = t.astype(jnp.float32)

    def cs(pos):
        ang = pos[:, None] * inv[None, :]
        return jnp.cos(ang), jnp.sin(ang)

    ones = lambda n: jnp.ones((S, n), jnp.float32)
    zeros = lambda n: jnp.zeros((S, n), jnp.float32)
    ct, st = cs(tf)
    cr, sr = cs(row)
    cc, sc = cs(col)
    tabf = jnp.concatenate([ct * MLA_QSCALE, st * MLA_QSCALE,
                            cr * GQA_QSCALE, sr * GQA_QSCALE,
                            cc * GQA_QSCALE, sc * GQA_QSCALE], axis=1).T
    mla = (jnp.concatenate([ones(64), ct, ct, ones(32)], 1),
           jnp.concatenate([zeros(64), -st, zeros(16), zeros(32)], 1),
           jnp.concatenate([zeros(64), zeros(16), st, zeros(32)], 1))
    ax = (jnp.concatenate([cr, cr, cc, cc, ones(64)], 1),
          jnp.concatenate([-sr, zeros(16), -sc, zeros(16), zeros(64)], 1),
          jnp.concatenate([zeros(16), sr, zeros(16), sc, zeros(64)], 1))
    tabt = jnp.concatenate(list(mla) + list(ax), axis=1)
    return tabf, tabt


def _np_in_perms():
    zc = 2208
    f = np.concatenate([np.arange(0, 256),
                        np.arange(256, 384),
                        np.arange(416, 928),
                        np.arange(1696, 2208),
                        np.arange(928, 1440),
                        np.arange(1568, 1696)]).astype(np.int32)
    t = np.full((N_T,), zc, np.int32)
    t[T_KVLAT:T_KVLAT + 128] = np.arange(256, 384)
    t[T_KROPE + 64:T_KROPE + 96] = np.arange(384, 416)
    for h in range(GQA_KV_HEADS):
        t[T_KB + h * LANES:T_KB + h * LANES + 64] = np.arange(1440 + 64 * h, 1440 + 64 * h + 64)
    return f, t


def _np_ukv_perm():
    zc = MLA_HEADS * (MLA_NOPE + MLA_V)
    ik = np.full((MLA_HEADS * LANES,), zc, np.int32)
    iv = np.zeros((MLA_HEADS * MLA_V,), np.int32)
    for h in range(MLA_HEADS):
        ik[h * LANES:h * LANES + 64] = np.arange(h * 128, h * 128 + 64)
        iv[h * 64:(h + 1) * 64] = np.arange(h * 128 + 64, h * 128 + 128)
    return ik, iv


def _take_cols(w, idx):
    wz = jnp.concatenate([w, jnp.zeros((w.shape[0], 1), w.dtype)], axis=1)
    return jnp.take(wz, jnp.asarray(idx), axis=1)


def _layer_weights(l, w_in, g_norm, g_qa, w_uq, g_kva, w_ukv, g_qn, g_kn, w_out):
    bf = jnp.bfloat16
    i_f, i_t = _np_in_perms()
    ik, iv = _np_ukv_perm()
    pad64 = lambda g: jnp.concatenate([g, jnp.zeros((LANES - GQA_DIM,), g.dtype)])[None, :]
    return {
        "g_norm": g_norm[l][None, :],
        "w_f": _take_cols(w_in[l], i_f).T.astype(bf),
        "w_t": _take_cols(w_in[l], i_t).astype(bf),
        "g_qa": g_qa[l][:, None],
        "w_uq": w_uq[l].T.astype(bf),
        "g_kva_t": g_kva[l][None, :],
        "g_kva_f": g_kva[l][:, None],
        "w_uk": _take_cols(w_ukv[l], ik).astype(bf),
        "w_uv": _take_cols(w_ukv[l], iv).T.astype(bf),
        "g_qn_f": g_qn[l][:, None],
        "g_kn_t": pad64(g_kn[l]),
        "w_out": w_out[l].astype(bf),
    }


def _attn_tiles(S):
    n_chunks = S // TOKEN_TILE
    unroll = min(n_chunks, MAX_CHUNK_UNROLL)
    return 4 * MXU_N, MXU_N, unroll


def _trunk(x, mods, lws, g_final):
    B, S, D = x.shape
    tq_a, tq_b, unroll = _attn_tiles(S)
    tabf, tabt = _rope_tables(S)
    for l in range(DEPTH):
        mod = mods[l]
        lw = lws[l]
        qa_t, ka, va_t, z_t, qb_t, kb, vb_t = _proj_call(x, mod, lw, tabf, tabt)
        oa_t = _attn_call(qa_t, ka, va_t, 1, tq_a, unroll, "mla_attn")
        ob_t = _attn_call(qb_t, kb, vb_t, GQA_GROUP, tq_b, unroll, "gqa_attn")
        x = _out_call(x, oa_t, ob_t, z_t, mod, lw["w_out"])
    return _final_call(x, g_final[None, :])


def kernel(x_prompt, x_sample, c_prompt, c_sample, w_ada, b_ada, g_norm, w_in, g_qa, w_uq,
           g_kva, w_ukv, g_qn, g_kn, w_out, g_final):
    nb_p, nb_s = c_prompt.shape[0], c_sample.shape[0]
    c_pad = jnp.concatenate(
        [c_prompt, c_sample, jnp.zeros((8 - nb_p - nb_s, D_MODEL), jnp.float32)], axis=0)
    mod_all = _ada_call(c_pad, w_ada.astype(jnp.bfloat16), b_ada[:, None, :])

    def mods_for(lo, n):
        out = []
        for l in range(DEPTH):
            m = mod_all[l, lo:lo + n].reshape(n, 3, D_MODEL)
            out.append(jnp.concatenate([m, jnp.zeros((n, 5, D_MODEL), jnp.float32)], axis=1))
        return out

    lws = [_layer_weights(l, w_in, g_norm, g_qa, w_uq, g_kva, w_ukv, g_qn, g_kn, w_out)
           for l in range(DEPTH)]
    y_p = _trunk(x_prompt, mods_for(0, nb_p), lws, g_final)
    y_s = _trunk(x_sample, mods_for(nb_p, nb_s), lws, g_final)
    return (y_p, y_s)
```

```python
import functools
import math

import numpy as np
import jax
import jax.numpy as jnp
from jax import lax
from jax.experimental import pallas as pl
from jax.experimental.pallas import tpu as pltpu

D_MODEL = 1024
DEPTH = 4
GRID_W = 64
EPS = 1e-6
ROPE_THETA = 10000.0
MLA_HEADS = 8
MLA_Q_RANK = 256
MLA_KV_RANK = 128
MLA_NOPE = 64
MLA_ROPE = 32
MLA_V = 64
MLA_QK = MLA_NOPE + MLA_ROPE
GQA_HEADS = 8
GQA_KV_HEADS = 2
GQA_DIM = 64
GQA_GROUP = GQA_HEADS // GQA_KV_HEADS
HEAD_V = 64
MLA_WIDTH = MLA_HEADS * MLA_V
GQA_WIDTH = GQA_HEADS * GQA_DIM
D_MIX = MLA_WIDTH + GQA_WIDTH
ROPE_HALF = 16
LANES = 128
SUBLANES = 8
MXU_N = 256
LOG2E = math.log2(math.e)
MLA_QSCALE = LOG2E / math.sqrt(MLA_QK)
GQA_QSCALE = LOG2E / math.sqrt(GQA_DIM)
NEG_BIG = -1e30

F_QLAT = 0
F_KVLAT = F_QLAT + MLA_Q_RANK
F_Z = F_KVLAT + MLA_KV_RANK
F_QB = F_Z + D_MIX
F_VB = F_QB + GQA_HEADS * GQA_DIM
F_KB = F_VB + GQA_KV_HEADS * GQA_DIM
F_KROPE = F_KB + GQA_KV_HEADS * GQA_DIM
N_F = F_KROPE + MLA_ROPE
R_MLA_C, R_MLA_S, R_ROW_C, R_ROW_S, R_COL_C, R_COL_S = (16 * i for i in range(6))
N_TAB_Q = 96
N_TAB = 2 * N_TAB_Q

VMEM_LIMIT = 56 * 1024 * 1024
TOKEN_TILE = 512
PROJ_SUB_TILE = 256
KEY_CHUNK = 256
LOOKAHEAD = 6
SCORE_SLOTS = 8
BF16_ROWS = 16
HEAD_VX = HEAD_V + BF16_ROWS
STREAMS_PER_STEP = 32
MAX_ITEMS_PER_REGION = 128


def _cparams(sem):
    return pltpu.CompilerParams(dimension_semantics=sem, vmem_limit_bytes=VMEM_LIMIT)


def _nt_dot(a, b):
    return lax.dot_general(a, b, (((1,), (1,)), ((), ())), preferred_element_type=jnp.float32)


def _tn_dot(a, b):
    return lax.dot_general(a, b, (((0,), (0,)), ((), ())), preferred_element_type=jnp.float32)


def _ada_kernel(c_ref, w_ref, b_ref, o_ref):
    c = c_ref[...]
    sc = (c * jax.nn.sigmoid(c)).astype(jnp.bfloat16)
    o_ref[0] = jnp.dot(sc, w_ref[0].astype(jnp.bfloat16),
                       preferred_element_type=jnp.float32) + b_ref[0]


def _ada_call(c_pad, w_ada, b_ada):
    n = 3 * D_MODEL
    tn = 1024
    return pl.pallas_call(
        _ada_kernel,
        grid=(DEPTH, n // tn),
        in_specs=[pl.BlockSpec((SUBLANES, D_MODEL), lambda l, j: (0, 0)),
                  pl.BlockSpec((1, D_MODEL, tn), lambda l, j: (l, 0, j)),
                  pl.BlockSpec((1, 1, tn), lambda l, j: (l, 0, j))],
        out_specs=pl.BlockSpec((1, SUBLANES, tn), lambda l, j: (l, 0, j)),
        out_shape=jax.ShapeDtypeStruct((DEPTH, SUBLANES, n), jnp.float32),
        compiler_params=_cparams(("arbitrary", "arbitrary")),
        name="adaln",
    )(c_pad, w_ada, b_ada)


def _out_body(x_ref, oa_ref, ob_ref, z_ref, mod_ref, w_ref):
    ya = oa_ref[0] * z_ref[0, 0:MLA_WIDTH, :]
    yb = ob_ref[0] * z_ref[0, MLA_WIDTH:D_MIX, :]
    out = _tn_dot(ya, w_ref[0:MLA_WIDTH, :]) + _tn_dot(yb, w_ref[MLA_WIDTH:D_MIX, :])
    return x_ref[0] + mod_ref[0, 2:3, :] * out


def _proj_sub_tiles(x, refs):
    ins, tab_ref, outs = refs[:N_PROJ_IN - 1], refs[N_PROJ_IN - 1], refs[N_PROJ_IN:]
    qa_ref, ka_ref, va_ref, z_ref, qb_ref, kb_ref, vb_ref = outs
    n = PROJ_SUB_TILE
    for t0 in range(0, x.shape[0], n):
        tok = slice(t0, t0 + n)
        chunks = slice(t0 // KEY_CHUNK, (t0 + n) // KEY_CHUNK)
        _proj_body(x[tok], *ins, tab_ref.at[:, tok],
                   qa_ref.at[:, :, tok], ka_ref.at[:, tok, :], va_ref.at[:, chunks],
                   z_ref.at[:, :, tok], qb_ref.at[:, :, tok], kb_ref.at[:, tok, :],
                   vb_ref.at[:, chunks])


def _proj_kernel(x_ref, *refs):
    _proj_sub_tiles(x_ref[0], refs)


N_PROJ_IN = 11


def _out_proj_kernel(x_ref, oa_ref, ob_ref, z_ref, modo_ref, wo_ref, *refs):
    proj_in, y_ref, proj_out = refs[:N_PROJ_IN], refs[N_PROJ_IN], refs[N_PROJ_IN + 1:]
    y = _out_body(x_ref, oa_ref, ob_ref, z_ref, modo_ref, wo_ref)
    y_ref[0] = y
    _proj_sub_tiles(y, (*proj_in, *proj_out))


def _rope_rows(x1, x2, c, s):
    return x1 * c - x2 * s, x1 * s + x2 * c


def _proj_body(x, mod_ref, gn_ref, wf_ref, gqa_ref, wuq_ref, gkva_ref, wuk_ref, wuv_ref,
               gqn_ref, gkn_ref, tab_ref,
               qa_ref, ka_ref, va_ref, z_ref, qb_ref, kb_ref, vb_ref):
    bf = jnp.bfloat16
    tm = x.shape[0]
    shift = mod_ref[0, 0:1, :]
    scale = mod_ref[0, 1:2, :]
    ms = jnp.mean(x * x, axis=-1, keepdims=True)
    h = (x * lax.rsqrt(ms + EPS)) * gn_ref[...]
    h = (h * (1.0 + scale) + shift).astype(bf)

    def proj_f(lo, n):
        return _nt_dot(wf_ref[lo:lo + n, :], h)

    def tab(lo, for_q):
        lo = lo + (0 if for_q else N_TAB_Q)
        return tab_ref[lo:lo + ROPE_HALF, :]

    kb = proj_f(F_KB, GQA_KV_HEADS * GQA_DIM)
    kr = proj_f(F_KROPE, MLA_ROPE)
    ql = proj_f(F_QLAT, MLA_Q_RANK)
    kvl = proj_f(F_KVLAT, MLA_KV_RANK)
    qb = proj_f(F_QB, GQA_HEADS * GQA_DIM)
    vb = proj_f(F_VB, GQA_KV_HEADS * GQA_DIM).astype(bf)
    z0 = proj_f(F_Z, MLA_WIDTH)
    qn = (ql * lax.rsqrt(jnp.mean(ql * ql, axis=0, keepdims=True) + EPS)) * gqa_ref[...]
    kvn = ((kvl * lax.rsqrt(jnp.mean(kvl * kvl, axis=0, keepdims=True) + EPS))
           * gkva_ref[...]).astype(bf)
    qa = jnp.dot(wuq_ref[...], qn.astype(bf), preferred_element_type=jnp.float32)
    kn = jnp.dot(wuk_ref[...], kvn, preferred_element_type=jnp.float32)
    va = jnp.dot(wuv_ref[...], kvn, preferred_element_type=jnp.float32).astype(bf)
    z1 = proj_f(F_Z + MLA_WIDTH, GQA_WIDTH)

    def tabf(lo):
        return tab(lo, True)

    cm, sm = tabf(R_MLA_C), tabf(R_MLA_S)
    zpad = jnp.zeros((LANES - MLA_QK, x.shape[0]), bf)
    for hh in range(MLA_HEADS):
        src = hh * MLA_QK
        dst = hh * LANES
        qa_ref[0, dst:dst + MLA_NOPE, :] = (qa[src:src + MLA_NOPE] * MLA_QSCALE).astype(bf)
        x1 = qa[src + MLA_NOPE:src + MLA_NOPE + ROPE_HALF]
        x2 = qa[src + MLA_NOPE + ROPE_HALF:src + MLA_QK]
        qa_ref[0, dst + MLA_NOPE:dst + MLA_NOPE + ROPE_HALF, :] = (x1 * cm - x2 * sm).astype(bf)
        qa_ref[0, dst + MLA_NOPE + ROPE_HALF:dst + MLA_QK, :] = (x1 * sm + x2 * cm).astype(bf)
        qa_ref[0, dst + MLA_QK:dst + LANES, :] = zpad

    row = lax.broadcasted_iota(jnp.int32, (HEAD_VX - HEAD_V, KEY_CHUNK), 0)
    ones_rows = jnp.where(row == 0, 1.0, 0.0).astype(bf)

    def store_v(v_ref, v, n_heads):
        for cj in range(x.shape[0] // KEY_CHUNK):
            cols = slice(cj * KEY_CHUNK, (cj + 1) * KEY_CHUNK)
            for hh in range(n_heads):
                v_ref[0, cj, hh * HEAD_VX:hh * HEAD_VX + HEAD_V, :] = v[hh * HEAD_V:(hh + 1) * HEAD_V, cols]
                v_ref[0, cj, hh * HEAD_VX + HEAD_V:(hh + 1) * HEAD_VX, :] = ones_rows

    store_v(va_ref, va, MLA_HEADS)

    kp1, kp2 = _rope_rows(kr[0:ROPE_HALF], kr[ROPE_HALF:MLA_ROPE],
                          tab(R_MLA_C, False), tab(R_MLA_S, False))
    ktail = jnp.concatenate([kp1, kp2, jnp.zeros((LANES - MLA_QK, tm), jnp.float32)], axis=0)
    for hh in range(MLA_HEADS):
        blk = jnp.concatenate([kn[hh * MLA_NOPE:(hh + 1) * MLA_NOPE], ktail], axis=0)
        ka_ref[0, :, hh * LANES:(hh + 1) * LANES] = blk.T.astype(bf)

    z_ref[0, 0:MLA_WIDTH, :] = (z0 * jax.nn.sigmoid(z0)).astype(bf)
    z_ref[0, MLA_WIDTH:D_MIX, :] = (z1 * jax.nn.sigmoid(z1)).astype(bf)

    gqn = gqn_ref[...]
    cr, sr, cc, sc = tabf(R_ROW_C), tabf(R_ROW_S), tabf(R_COL_C), tabf(R_COL_S)
    zpad = jnp.zeros((LANES - GQA_DIM, x.shape[0]), bf)
    for hh in range(GQA_HEADS):
        xh = qb[hh * GQA_DIM:(hh + 1) * GQA_DIM]
        r = lax.rsqrt(jnp.mean(xh * xh, axis=0, keepdims=True) + EPS)
        xn = xh * r * gqn
        dst = hh * LANES
        for j, (c, s) in enumerate(((cr, sr), (cc, sc))):
            x1 = xn[32 * j:32 * j + ROPE_HALF]
            x2 = xn[32 * j + ROPE_HALF:32 * j + 32]
            qb_ref[0, dst + 32 * j:dst + 32 * j + ROPE_HALF, :] = (x1 * c - x2 * s).astype(bf)
            qb_ref[0, dst + 32 * j + ROPE_HALF:dst + 32 * j + 32, :] = (x1 * s + x2 * c).astype(bf)
        qb_ref[0, dst + GQA_DIM:dst + LANES, :] = zpad

    gkn = gkn_ref[...]
    kpad = jnp.zeros((LANES - GQA_DIM, tm), jnp.float32)
    for hh in range(GQA_KV_HEADS):
        xh = kb[hh * GQA_DIM:(hh + 1) * GQA_DIM]
        r = lax.rsqrt(jnp.mean(xh * xh, axis=0, keepdims=True) + EPS)
        xn = xh * r * gkn
        parts = []
        for j, (c, s) in enumerate(((R_ROW_C, R_ROW_S), (R_COL_C, R_COL_S))):
            parts.extend(_rope_rows(xn[32 * j:32 * j + ROPE_HALF], xn[32 * j + ROPE_HALF:32 * j + 32],
                                    tab(c, False), tab(s, False)))
        blk = jnp.concatenate(parts + [kpad], axis=0)
        kb_ref[0, :, hh * LANES:(hh + 1) * LANES] = blk.T.astype(bf)
    store_v(vb_ref, vb, GQA_KV_HEADS)


def _proj_call(x, mod, lw, tab, prev=None):
    B, S, D = x.shape
    tm = TOKEN_TILE
    nt = S // tm
    bf = jnp.bfloat16
    full = lambda a: pl.BlockSpec(a.shape, lambda b, i: (0,) * a.ndim)
    tok = lambda n: pl.BlockSpec((1, tm, n), lambda b, i: (b, i, 0))
    feat = lambda n: pl.BlockSpec((1, n, tm), lambda b, i: (b, 0, i))
    per = tm // KEY_CHUNK
    chunked = lambda n: pl.BlockSpec((1, per, n, KEY_CHUNK), lambda b, i: (b, i, 0, 0))
    modspec = pl.BlockSpec((1, SUBLANES, D), lambda b, i: (b, 0, 0))
    names = ["g_norm", "w_f", "g_qa", "w_uq", "g_kva", "w_uk", "w_uv", "g_qn", "g_kn"]
    proj_args = [mod] + [lw[n] for n in names] + [tab]
    proj_specs = ([modspec] + [full(lw[n]) for n in names]
                  + [pl.BlockSpec((N_TAB, tm), lambda b, i: (0, i))])
    assert len(proj_args) == N_PROJ_IN
    out_specs = [feat(MLA_HEADS * LANES), tok(MLA_HEADS * LANES), chunked(MLA_HEADS * HEAD_VX),
                 feat(D_MIX), feat(GQA_HEADS * LANES), tok(GQA_KV_HEADS * LANES),
                 chunked(GQA_KV_HEADS * HEAD_VX)]
    out_shape = [jax.ShapeDtypeStruct((B, MLA_HEADS * LANES, S), bf),
                 jax.ShapeDtypeStruct((B, S, MLA_HEADS * LANES), bf),
                 jax.ShapeDtypeStruct((B, nt * per, MLA_HEADS * HEAD_VX, KEY_CHUNK), bf),
                 jax.ShapeDtypeStruct((B, D_MIX, S), bf),
                 jax.ShapeDtypeStruct((B, GQA_HEADS * LANES, S), bf),
                 jax.ShapeDtypeStruct((B, S, GQA_KV_HEADS * LANES), bf),
                 jax.ShapeDtypeStruct((B, nt * per, GQA_KV_HEADS * HEAD_VX, KEY_CHUNK), bf)]
    if prev is None:
        outs = pl.pallas_call(
            _proj_kernel, grid=(B, nt),
            in_specs=[tok(D)] + proj_specs, out_specs=out_specs, out_shape=out_shape,
            compiler_params=_cparams(("parallel", "parallel")), name="proj",
        )(x, *proj_args)
        return x, outs
    oa_t, ob_t, z_t, mod_prev, w_out_prev = prev
    outs = pl.pallas_call(
        _out_proj_kernel, grid=(B, nt),
        in_specs=[tok(D), feat(MLA_WIDTH), feat(GQA_WIDTH), feat(D_MIX), modspec, full(w_out_prev)] + proj_specs,
        out_specs=[tok(D)] + out_specs,
        out_shape=[jax.ShapeDtypeStruct((B, S, D), jnp.float32)] + out_shape,
        compiler_params=_cparams(("parallel", "parallel")), name="out_proj_proj",
    )(x, oa_t, ob_t, z_t, mod_prev, w_out_prev, *proj_args)
    return outs[0], outs[1:]


def _attn_kernel(trips_ref, q_ref, k_ref, v_ref, o_ref, m_sc, acc_sc, s_sc, cm_sc, *,
                 group, unroll):
    tq = q_ref.shape[2]
    tk = v_ref.shape[3]
    n_chunks = v_ref.shape[1]
    n_col = tq // MXU_N
    n_stream = group * n_col
    n_items = unroll * n_stream
    n_iter = n_chunks // unroll

    m_sc[...] = jnp.full(m_sc.shape, NEG_BIG, jnp.float32)
    acc_sc[...] = jnp.zeros(acc_sc.shape, jnp.float32)

    def scores(c0, i):
        u, st = divmod(i, n_stream)
        g, j = divmod(st, n_col)
        k_c = k_ref[0, pl.ds(pl.multiple_of((c0 + u) * tk, tk), tk), :]
        q_t = q_ref[0, g * LANES:(g + 1) * LANES, j * MXU_N:(j + 1) * MXU_N]
        s = jnp.dot(k_c, q_t, preferred_element_type=jnp.float32)
        s_sc[i % SCORE_SLOTS] = s
        cm_sc[i % SCORE_SLOTS] = jnp.max(s, axis=0, keepdims=True)

    def update(c0, i):
        u, st = divmod(i, n_stream)
        m_prev = m_sc[st]
        m_new = jnp.maximum(m_prev, cm_sc[i % SCORE_SLOTS])
        alpha = jnp.exp2(m_prev - m_new)
        p = jnp.exp2(s_sc[i % SCORE_SLOTS] - m_new).astype(jnp.bfloat16)
        m_sc[st] = m_new
        pv = jnp.dot(v_ref[0, c0 + u], p, preferred_element_type=jnp.float32)
        acc_sc[st] = acc_sc[st] * alpha + pv

    def iteration(c0, last):
        for i in range(n_items):
            nx = i + LOOKAHEAD
            if nx < n_items:
                scores(c0, nx)
            elif not last:
                scores(c0 + unroll, nx - n_items)
            update(c0, i)

    for i in range(LOOKAHEAD):
        scores(0, i)
    if n_iter > 1:
        def step(it, carry):
            iteration(it * unroll, False)
            return carry
        lax.fori_loop(0, trips_ref[0], step, 0)
    iteration((n_iter - 1) * unroll, True)

    for st in range(n_stream):
        g, j = divmod(st, n_col)
        acc = acc_sc[st]
        o_ref[0, g * HEAD_V:(g + 1) * HEAD_V, j * MXU_N:(j + 1) * MXU_N] = (
            acc[0:HEAD_V] / acc[HEAD_V:HEAD_V + 1]).astype(o_ref.dtype)


def _attn_call(q_t, k, v_t, group, tq, unroll, name):
    B, _, S = q_t.shape
    n_kv = k.shape[2] // LANES
    n_chunks, tk = v_t.shape[1], v_t.shape[3]
    n_stream = group * (tq // MXU_N)
    assert n_chunks % unroll == 0 and (unroll * n_stream) % SCORE_SLOTS == 0
    assert LOOKAHEAD < SCORE_SLOTS <= unroll * n_stream and S % tq == 0
    trips = jnp.full((1,), n_chunks // unroll - 1, jnp.int32)
    grid_spec = pltpu.PrefetchScalarGridSpec(
        num_scalar_prefetch=1,
        grid=(B, n_kv, S // tq),
        in_specs=[pl.BlockSpec((1, group * LANES, tq), lambda b, h, i, t: (b, h, i)),
                  pl.BlockSpec((1, S, LANES), lambda b, h, i, t: (b, 0, h)),
                  pl.BlockSpec((1, n_chunks, HEAD_VX, tk), lambda b, h, i, t: (b, 0, h, 0))],
        out_specs=pl.BlockSpec((1, group * HEAD_V, tq), lambda b, h, i, t: (b, h, i)),
        scratch_shapes=[pltpu.VMEM((n_stream, 1, MXU_N), jnp.float32),
                        pltpu.VMEM((n_stream, HEAD_VX, MXU_N), jnp.float32),
                        pltpu.VMEM((SCORE_SLOTS, tk, MXU_N), jnp.float32),
                        pltpu.VMEM((SCORE_SLOTS, 1, MXU_N), jnp.float32)])
    return pl.pallas_call(
        functools.partial(_attn_kernel, group=group, unroll=unroll),
        grid_spec=grid_spec,
        out_shape=jax.ShapeDtypeStruct((B, n_kv * group * HEAD_V, S), jnp.bfloat16),
        compiler_params=_cparams(("parallel", "parallel", "parallel")),
        name=name,
    )(trips, q_t, k, v_t)


def _out_final_kernel(x_ref, oa_ref, ob_ref, z_ref, mod_ref, w_ref, gf_ref, y_ref):
    y = _out_body(x_ref, oa_ref, ob_ref, z_ref, mod_ref, w_ref)
    y_ref[0] = (y * lax.rsqrt(jnp.mean(y * y, axis=-1, keepdims=True) + EPS)) * gf_ref[...]


def _out_final_call(x, oa_t, ob_t, z_t, mod, w_out, g_final):
    B, S, D = x.shape
    tm = 2 * TOKEN_TILE
    tok = lambda n: pl.BlockSpec((1, tm, n), lambda b, i: (b, i, 0))
    feat = lambda n: pl.BlockSpec((1, n, tm), lambda b, i: (b, 0, i))
    return pl.pallas_call(
        _out_final_kernel,
        grid=(B, S // tm),
        in_specs=[tok(D), feat(MLA_WIDTH), feat(GQA_WIDTH), feat(D_MIX),
                  pl.BlockSpec((1, SUBLANES, D), lambda b, i: (b, 0, 0)),
                  pl.BlockSpec(w_out.shape, lambda b, i: (0, 0)),
                  pl.BlockSpec((1, D), lambda b, i: (0, 0))],
        out_specs=tok(D),
        out_shape=jax.ShapeDtypeStruct((B, S, D), jnp.float32),
        compiler_params=_cparams(("parallel", "parallel")),
        name="out_proj",
    )(x, oa_t, ob_t, z_t, mod, w_out, g_final)


def _rope_tables(S):
    inv = ROPE_THETA ** (-jnp.arange(ROPE_HALF, dtype=jnp.float32) / ROPE_HALF)
    t = jnp.arange(S, dtype=jnp.int32)
    row = (t // GRID_W).astype(jnp.float32)
    col = (t % GRID_W).astype(jnp.float32)
    tf = t.astype(jnp.float32)

    def cs(pos):
        ang = pos[:, None] * inv[None, :]
        return jnp.cos(ang), jnp.sin(ang)

    ct, st = cs(tf)
    cr, sr = cs(row)
    cc, sc = cs(col)
    base = jnp.concatenate([ct, st, cr, sr, cc, sc], axis=1)
    qscale = np.repeat(np.array([MLA_QSCALE] * 2 + [GQA_QSCALE] * 4, np.float32), ROPE_HALF)
    return jnp.concatenate([base * qscale[None, :], base], axis=1).T


def _layer_weights(l, w_in, g_norm, g_qa, w_uq, g_kva, w_ukv, g_qn, g_kn, w_out):
    bf = jnp.bfloat16
    w = w_in[l]
    q_lat, kv_lat, k_rope = w[:, 0:256], w[:, 256:384], w[:, 384:416]
    z_a, q_b, k_b, v_b, z_b = w[:, 416:928], w[:, 928:1440], w[:, 1440:1568], w[:, 1568:1696], w[:, 1696:2208]
    w_f = jnp.concatenate([q_lat, kv_lat, z_a, z_b, q_b, v_b, k_b, k_rope], axis=1)
    ukv = w_ukv[l].reshape(MLA_KV_RANK, MLA_HEADS, MLA_NOPE + MLA_V)
    return {
        "g_norm": g_norm[l][None, :],
        "w_f": w_f.T.astype(bf),
        "g_qa": g_qa[l][:, None],
        "w_uq": w_uq[l].T.astype(bf),
        "g_kva": g_kva[l][:, None],
        "w_uk": ukv[:, :, :MLA_NOPE].reshape(MLA_KV_RANK, MLA_HEADS * MLA_NOPE).T.astype(bf),
        "w_uv": ukv[:, :, MLA_NOPE:].reshape(MLA_KV_RANK, MLA_HEADS * MLA_V).T.astype(bf),
        "g_qn": g_qn[l][:, None],
        "g_kn": g_kn[l][:, None],
        "w_out": w_out[l].astype(bf),
    }


def _attn_tiles(S):
    n_chunks = S // KEY_CHUNK
    n_stream = min(STREAMS_PER_STEP, S // MXU_N)
    unroll = min(max(n_chunks // 2, 1), MAX_ITEMS_PER_REGION // n_stream)
    return n_stream * MXU_N, (n_stream // GQA_GROUP) * MXU_N, unroll


def _trunk(x, mods, lws, g_final, tab):
    B, S, D = x.shape
    tq_a, tq_b, unroll = _attn_tiles(S)
    prev = None
    for l in range(DEPTH):
        mod = mods[l]
        lw = lws[l]
        x, (qa_t, ka, va_t, z_t, qb_t, kb, vb_t) = _proj_call(x, mod, lw, tab, prev)
        oa_t = _attn_call(qa_t, ka, va_t, 1, tq_a, unroll, "mla_attn")
        ob_t = _attn_call(qb_t, kb, vb_t, GQA_GROUP, tq_b, unroll, "gqa_attn")
        prev = (oa_t, ob_t, z_t, mod, lw["w_out"])
    return _out_final_call(x, *prev, g_final[None, :])


def kernel(x_prompt, x_sample, c_prompt, c_sample, w_ada, b_ada, g_norm, w_in, g_qa, w_uq,
           g_kva, w_ukv, g_qn, g_kn, w_out, g_final):
    nb_p, nb_s = c_prompt.shape[0], c_sample.shape[0]
    c_pad = jnp.concatenate(
        [c_prompt, c_sample, jnp.zeros((SUBLANES - nb_p - nb_s, D_MODEL), jnp.float32)], axis=0)
    mod_all = _ada_call(c_pad, w_ada, b_ada[:, None, :])

    def mods_for(lo, n):
        out = []
        for l in range(DEPTH):
            m = mod_all[l, lo:lo + n].reshape(n, 3, D_MODEL)
            out.append(jnp.concatenate([m, jnp.zeros((n, SUBLANES - 3, D_MODEL), jnp.float32)], axis=1))
        return out

    lws = [_layer_weights(l, w_in, g_norm, g_qa, w_uq, g_kva, w_ukv, g_qn, g_kn, w_out)
           for l in range(DEPTH)]
    tab = _rope_tables(max(x_prompt.shape[1], x_sample.shape[1]))
    y_p = _trunk(x_prompt, mods_for(0, nb_p), lws, g_final, tab)
    y_s = _trunk(x_sample, mods_for(nb_p, nb_s), lws, g_final, tab)
    return (y_p, y_s)
```
